```python
import jax, jax.numpy as jnp
from jax import lax
import numpy as np

D_MODEL = 1024
BATCH = 8
SEQ = 2048
DEPTH = 2
DEC_BATCH = 2
DEC_SEQ = 16384
PAST_LEN = 128

N_MIXERS = 2
N_META = 16
D_FF = 2816
CONV_W = 3
HG_HEADS = 8
HG_DK = D_MODEL // HG_HEADS
HG_DV = D_MODEL // HG_HEADS
D_HG = HG_HEADS * HG_DK
CHUNK = 64
EPS = 1e-6
N_CONV_LAYERS = (DEPTH + 1) // 2
N_HGRN_LAYERS = DEPTH // 2

kernel_name = "hybrid_shortconv_hgrn2_macaron_encoder"


def rmsnorm(x, g):
    xf = x.astype(jnp.float32)
    y = xf * lax.rsqrt(jnp.mean(xf * xf, axis=-1, keepdims=True) + EPS)
    return (y * g.astype(jnp.float32)).astype(x.dtype)


def swiglu(x, w_gate, w_up, w_down):
    return (jax.nn.silu(x @ w_gate) * (x @ w_up)) @ w_down


def short_conv_mixer(u, w_in, w_conv, w_out):
    seq_len = u.shape[1]
    gate_b, gate_c, xv = jnp.split(u @ w_in, 3, axis=-1)
    z = gate_c * xv
    zp = jnp.pad(z, ((0, 0), (1, 1), (0, 0)))
    conv = (w_conv[0] * zp[:, :seq_len] + w_conv[1] * zp[:, 1:seq_len + 1]
            + w_conv[2] * zp[:, 2:])
    return (gate_b * conv) @ w_out


def chunk_gla(q, k, v, logf, s0):
    b = jnp.cumsum(logf, axis=3)
    c = q.shape[3]
    lower_tri = jnp.tril(jnp.ones((c, c), dtype=bool))[:, :, None]

    def step(s, inp):
        qc, kc, vc, bc = inp
        o_inter = jnp.einsum('bhtk,bhkv->bhtv', qc * jnp.exp(bc), s)
        diff = bc[:, :, :, None, :] - bc[:, :, None, :, :]
        decay = jnp.exp(jnp.where(lower_tri, diff, -jnp.inf))
        scores = jnp.einsum('bhtk,bhsk,bhtsk->bhts', qc, kc, decay)
        o_intra = jnp.einsum('bhts,bhsv->bhtv', scores, vc)
        b_last = bc[:, :, -1:, :]
        s_new = (jnp.exp(b_last[:, :, 0, :])[..., None] * s
                 + jnp.einsum('bhsk,bhsv->bhkv', kc * jnp.exp(b_last - bc), vc))
        return s_new, o_inter + o_intra

    xs = tuple(jnp.moveaxis(t, 2, 0) for t in (q, k, v, b))
    s_fin, o = lax.scan(step, s0, xs)
    return jnp.moveaxis(o, 0, 2), s_fin


def _segment(t, start, length, c):
    bn, h, _, d = t.shape
    return t[:, :, start:start + length].reshape(bn, h, length // c, c, d)


def directional_scan(q, k, v, logf, segments):
    bn, h, _, dk = q.shape
    dv = v.shape[-1]
    s = jnp.zeros((bn, h, dk, dv), jnp.float32)
    outs = []
    start = 0
    for length, c in segments:
        o, s = chunk_gla(_segment(q, start, length, c), _segment(k, start, length, c),
                         _segment(v, start, length, c), _segment(logf, start, length, c), s)
        outs.append(o.reshape(bn, h, length, dv))
        start += length
    return jnp.concatenate(outs, axis=2)


def hgrn2_mixer(u, w_in, lb, gn, w_out):
    bn, seq_len, _ = u.shape
    n_real = seq_len - N_META
    q, v, z_fwd, z_bwd, g = jnp.split(u @ w_in, 5, axis=-1)

    def heads(t, d):
        return t.reshape(bn, seq_len, HG_HEADS, d).transpose(0, 2, 1, 3).astype(jnp.float32)

    qh = heads(jax.nn.silu(q), HG_DK)
    vh = heads(v, HG_DV)

    def gates(z, lbd):
        zh = heads(z, HG_DK)
        lbd = lbd.astype(jnp.float32).reshape(HG_HEADS, 1, HG_DK)
        logf = jnp.logaddexp(jnp.log(lbd), jnp.log1p(-lbd) + jax.nn.log_sigmoid(zh))
        key = (1.0 - lbd) * jax.nn.sigmoid(-zh)
        return key, logf

    k_f, lf_f = gates(z_fwd, lb[0])
    k_b, lf_b = gates(z_bwd, lb[1])
    o_fwd = directional_scan(qh, k_f, vh, lf_f, [(N_META, N_META), (n_real, CHUNK)])
    flip = lambda t: jnp.flip(t, axis=2)
    o_bwd = flip(directional_scan(flip(qh), flip(k_b), flip(vh), flip(lf_b),
                                  [(n_real, CHUNK), (N_META, N_META)]))
    o = (o_fwd + o_bwd).transpose(0, 2, 1, 3)
    gated = o * jax.nn.silu(g.astype(jnp.float32)).reshape(bn, seq_len, HG_HEADS, HG_DV)
    normed = gated * lax.rsqrt(jnp.mean(gated * gated, axis=-1, keepdims=True) + EPS)
    normed = normed * gn.astype(jnp.float32).reshape(HG_HEADS, HG_DV)
    return normed.reshape(bn, seq_len, D_HG).astype(u.dtype) @ w_out


def trunk(x, meta_tokens, norm_pre, norm_post, ffn_w_gate, ffn_w_up, ffn_w_down,
          sc_w_in, sc_conv, sc_w_out, hg_w_in, hg_lb_logits, hg_gn, hg_w_out, final_norm):
    bn = x.shape[0]
    meta = jnp.broadcast_to(meta_tokens.astype(x.dtype)[None], (bn, N_META, D_MODEL))
    h = jnp.concatenate([meta, x], axis=1)
    p = jax.nn.softmax(hg_lb_logits.astype(jnp.float32), axis=0)
    cs = jnp.cumsum(p, axis=0)
    lower_bounds = cs - cs[0:1]
    for i in range(DEPTH):
        f1 = swiglu(rmsnorm(h, norm_pre[i, 0]), ffn_w_gate[i, 0], ffn_w_up[i, 0], ffn_w_down[i, 0])
        h = h + 0.5 * rmsnorm(f1, norm_post[i, 0])
        u = rmsnorm(h, norm_pre[i, 1])
        j = i // N_MIXERS
        if i % N_MIXERS == 0:
            m = short_conv_mixer(u, sc_w_in[j], sc_conv[j], sc_w_out[j])
        else:
            m = hgrn2_mixer(u, hg_w_in[j], lower_bounds[i], hg_gn[j], hg_w_out[j])
        h = h + rmsnorm(m, norm_post[i, 1])
        f2 = swiglu(rmsnorm(h, norm_pre[i, 2]), ffn_w_gate[i, 1], ffn_w_up[i, 1], ffn_w_down[i, 1])
        h = h + 0.5 * rmsnorm(f2, norm_post[i, 2])
    h = rmsnorm(h, final_norm)
    return h[:, N_META:]


def setup_inputs(seed: int = 0) -> dict:
    key = jax.random.key(seed)
    ks = jax.random.split(key, 16)
    f32 = jnp.float32
    d = D_MODEL
    nrm = lambda k, shape, scale: jax.random.normal(k, shape, f32) * scale
    return {
        "x_prompt": nrm(ks[0], (BATCH, SEQ, d), 1.0),
        "x_sample": nrm(ks[1], (DEC_BATCH, DEC_SEQ, d), 1.0),
        "meta_tokens": nrm(ks[2], (N_META, d), 1.0),
        "norm_pre": 1.0 + nrm(ks[3], (DEPTH, 3, d), 0.02),
        "norm_post": 1.0 + nrm(ks[4], (DEPTH, 3, d), 0.02),
        "ffn_w_gate": nrm(ks[5], (DEPTH, 2, d, D_FF), d ** -0.5),
        "ffn_w_up": nrm(ks[6], (DEPTH, 2, d, D_FF), d ** -0.5),
        "ffn_w_down": nrm(ks[7], (DEPTH, 2, D_FF, d), D_FF ** -0.5),
        "sc_w_in": nrm(ks[8], (N_CONV_LAYERS, d, 3 * d), d ** -0.5),
        "sc_conv": nrm(ks[9], (N_CONV_LAYERS, CONV_W, d), CONV_W ** -0.5),
        "sc_w_out": nrm(ks[10], (N_CONV_LAYERS, d, d), d ** -0.5),
        "hg_w_in": nrm(ks[11], (N_HGRN_LAYERS, d, 5 * D_HG), d ** -0.5),
        "hg_lb_logits": nrm(ks[12], (DEPTH, 2, D_HG), 0.5),
        "hg_gn": 1.0 + nrm(ks[13], (N_HGRN_LAYERS, D_HG), 0.02),
        "hg_w_out": nrm(ks[14], (N_HGRN_LAYERS, D_HG, d), D_HG ** -0.5),
        "final_norm": 1.0 + nrm(ks[15], (d,), 0.02),
    }


def reference(x_prompt, x_sample, meta_tokens, norm_pre, norm_post, ffn_w_gate, ffn_w_up,
              ffn_w_down, sc_w_in, sc_conv, sc_w_out, hg_w_in, hg_lb_logits, hg_gn, hg_w_out,
              final_norm):
    y_prompt = trunk(x_prompt, meta_tokens, norm_pre, norm_post, ffn_w_gate, ffn_w_up, ffn_w_down,
                     sc_w_in, sc_conv, sc_w_out, hg_w_in, hg_lb_logits, hg_gn, hg_w_out, final_norm)
    y_sample = trunk(x_sample, meta_tokens, norm_pre, norm_post, ffn_w_gate, ffn_w_up, ffn_w_down,
                     sc_w_in, sc_conv, sc_w_out, hg_w_in, hg_lb_logits, hg_gn, hg_w_out, final_norm)
    return (y_prompt, y_sample)
```

```python
import functools

import numpy as np
import jax
import jax.numpy as jnp
from jax import lax
from jax.experimental import pallas as pl
from jax.experimental.pallas import tpu as pltpu

EPS = 1e-6
N_META = 16
HEAD_DIM = 128
CHUNK = 64
N_LEVELS = 6
NEG_BIG = -1e30
VMEM_LIMIT_BYTES = 56 * 1024 * 1024

F32 = jnp.float32
BF16 = jnp.bfloat16


def _rms(x, g):
    return x * lax.rsqrt(jnp.mean(x * x, axis=-1, keepdims=True) + EPS) * g


def _row_tile(n_rows):
    return 512 if n_rows >= 4096 else 128


def _resident(shape):
    nd = len(shape)
    return pl.BlockSpec(shape, lambda i, *_: (0,) * nd, pipeline_mode=pl.Buffered(1))


def _params():
    return pltpu.CompilerParams(dimension_semantics=("arbitrary",),
                                vmem_limit_bytes=VMEM_LIMIT_BYTES)


def _ffn_body(x_ref, gpre_ref, gpost_ref, wg_ref, wu_ref, wd_ref, *rest, n_split, final):
    o_ref = rest[-1]
    x = x_ref[...]
    u = _rms(x, gpre_ref[...]).astype(BF16)
    d_ff = wg_ref.shape[1]
    fc = d_ff // n_split
    f = None
    for c in range(n_split):
        a = jnp.dot(u, wg_ref[:, c * fc:(c + 1) * fc], preferred_element_type=F32)
        b = jnp.dot(u, wu_ref[:, c * fc:(c + 1) * fc], preferred_element_type=F32)
        hdn = (a * jax.nn.sigmoid(a) * b).astype(BF16)
        part = jnp.dot(hdn, wd_ref[c * fc:(c + 1) * fc, :], preferred_element_type=F32)
        f = part if f is None else f + part
    y = x + 0.5 * _rms(f, gpost_ref[...])
    if final:
        y = _rms(y, rest[0][...])
    o_ref[...] = y


def _ffn(x, gpre, gpost, wg, wu, wd, gfinal=None):
    nt, d = x.shape
    d_ff = wg.shape[1]
    tm = _row_tile(nt)
    n_split = 2 if d_ff % 256 == 0 and d_ff >= 1024 else 1
    row = pl.BlockSpec((tm, d), lambda i: (i, 0))
    vec = _resident((1, d))
    in_specs = [row, vec, vec, _resident((d, d_ff)), _resident((d, d_ff)), _resident((d_ff, d))]
    args = [x, gpre, gpost, wg, wu, wd]
    if gfinal is not None:
        in_specs.append(vec)
        args.append(gfinal)
    return pl.pallas_call(
        functools.partial(_ffn_body, n_split=n_split, final=gfinal is not None),
        grid=(nt // tm,),
        in_specs=in_specs,
        out_specs=row,
        out_shape=jax.ShapeDtypeStruct((nt, d), F32),
        compiler_params=_params(),
        name="ffn",
    )(*args)


def _conv_in_body(x_ref, gpre_ref, win_ref, gb_ref, z_ref):
    d = x_ref.shape[1]
    u = _rms(x_ref[...], gpre_ref[...]).astype(BF16)
    p = jnp.dot(u, win_ref[...], preferred_element_type=F32)
    gb_ref[...] = p[:, :d]
    z_ref[...] = p[:, d:2 * d] * p[:, 2 * d:]


def _conv_in(x, gpre, w_in):
    nt, d = x.shape
    tm = _row_tile(nt)
    row = pl.BlockSpec((tm, d), lambda i: (i, 0))
    return pl.pallas_call(
        _conv_in_body,
        grid=(nt // tm,),
        in_specs=[row, _resident((1, d)), _resident((d, 3 * d))],
        out_specs=[row, row],
        out_shape=[jax.ShapeDtypeStruct((nt, d), F32)] * 2,
        compiler_params=_params(),
        name="conv_in",
    )(x, gpre, w_in)


def _conv_out_body(so_ref, eo_ref, z_ref, zp_ref, zn_ref, gb_ref, h_ref, wc_ref, wout_ref,
                   gpost_ref, o_ref):
    i = pl.program_id(0)
    tm = z_ref.shape[0]
    z = z_ref[...]
    row = lax.broadcasted_iota(jnp.int32, (tm, 1), 0)
    zm1 = jnp.where(row == 0, zp_ref[7:8, :], pltpu.roll(z, 1, 0))
    zm1 = jnp.where(row == so_ref[i], 0.0, zm1)
    zp1 = jnp.where(row == tm - 1, zn_ref[0:1, :], pltpu.roll(z, tm - 1, 0))
    zp1 = jnp.where(row == eo_ref[i], 0.0, zp1)
    conv = wc_ref[0:1, :] * zm1 + wc_ref[1:2, :] * z + wc_ref[2:3, :] * zp1
    y = jnp.dot((gb_ref[...] * conv).astype(BF16), wout_ref[...], preferred_element_type=F32)
    o_ref[...] = h_ref[...] + _rms(y, gpost_ref[...])


def _conv_out(z, gb, h, w_conv, w_out, gpost, start_off, end_off):
    nt, d = z.shape
    tm = _row_tile(nt)
    sub = tm // 8
    n_sub = nt // 8
    row = pl.BlockSpec((tm, d), lambda i, *_: (i, 0))
    prev8 = pl.BlockSpec((8, d), lambda i, *_: (jnp.maximum(i * sub - 1, 0), 0))
    next8 = pl.BlockSpec((8, d), lambda i, *_: (jnp.minimum((i + 1) * sub, n_sub - 1), 0))
    grid_spec = pltpu.PrefetchScalarGridSpec(
        num_scalar_prefetch=2,
        grid=(nt // tm,),
        in_specs=[row, prev8, next8, row, row, _resident(w_conv.shape), _resident((d, d)),
                  _resident((1, d))],
        out_specs=row,
    )
    return pl.pallas_call(
        _conv_out_body,
        grid_spec=grid_spec,
        out_shape=jax.ShapeDtypeStruct((nt, d), F32),
        compiler_params=_params(),
        name="conv_out",
    )(start_off, end_off, z, z, z, gb, h, w_conv, w_out, gpost)


def _log_gates(z, lb):
    log_sig = jnp.minimum(z, 0.0) - jnp.log1p(jnp.exp(-jnp.abs(z)))
    a = jnp.log(lb)
    b = jnp.log1p(-lb) + log_sig
    logf = jnp.maximum(a, b) + jnp.log1p(jnp.exp(-jnp.abs(a - b)))
    key = (1.0 - lb) * jnp.exp(log_sig - z)
    return logf, key


def _split_hi_lo(x):
    hi = x.astype(BF16)
    lo = (x - hi.astype(F32)).astype(BF16)
    return hi, lo


def _hg_in_body(so_ref, eo_ref, x_ref, gpre_ref, win_ref, logit_ref,
                q_ref, v_ref, g_ref, kf_ref, kb_ref, lfh_ref, lfl_ref, lbh_ref, lbl_ref, *, layer):
    i = pl.program_id(0)
    tm, d = x_ref.shape
    u = _rms(x_ref[...], gpre_ref[...]).astype(BF16)
    p = jnp.dot(u, win_ref[...], preferred_element_type=F32)
    q, v, zf, zb, g = (p[:, j * d:(j + 1) * d] for j in range(5))
    q_ref[...] = (q * jax.nn.sigmoid(q)).astype(BF16)
    v_ref[...] = v.astype(BF16)
    g_ref[...] = (g * jax.nn.sigmoid(g)).astype(BF16)

    depth = logit_ref.shape[0] // 2
    row = lax.broadcasted_iota(jnp.int32, (tm, 1), 0)
    for direction, (zz, k_ref, hi_ref, lo_ref, off_ref) in enumerate(
            ((zf, kf_ref, lfh_ref, lfl_ref, so_ref), (zb, kb_ref, lbh_ref, lbl_ref, eo_ref))):
        logits = [logit_ref[2 * j + direction:2 * j + direction + 1, :] for j in range(depth)]
        m = functools.reduce(jnp.maximum, logits)
        e = [jnp.exp(l - m) for l in logits]
        lb = sum(e[1:layer + 1]) / sum(e)
        logf, key = _log_gates(zz, lb)
        logf = jnp.where(row == off_ref[i], NEG_BIG, logf)
        k_ref[...] = key.astype(BF16)
        hi, lo = _split_hi_lo(logf)
        hi_ref[...] = hi
        lo_ref[...] = lo


def _hg_in(x, gpre, w_in, logits, start_off, end_off, layer):
    nt, d = x.shape
    tm = _row_tile(nt)
    row = pl.BlockSpec((tm, d), lambda i, *_: (i, 0))
    grid_spec = pltpu.PrefetchScalarGridSpec(
        num_scalar_prefetch=2,
        grid=(nt // tm,),
        in_specs=[row, _resident((1, d)), _resident((d, 5 * d)), _resident(logits.shape)],
        out_specs=[row] * 9,
    )
    return pl.pallas_call(
        functools.partial(_hg_in_body, layer=layer),
        grid_spec=grid_spec,
        out_shape=[jax.ShapeDtypeStruct((nt, d), BF16)] * 9,
        compiler_params=_params(),
        name="hg_in",
    )(start_off, end_off, x, gpre, w_in, logits)


def _scan_tables():
    c = CHUNK
    w = np.zeros((N_LEVELS + 2, c, c), np.float32)
    level = np.full((c, c), N_LEVELS + 1, np.int32)
    for l in range(N_LEVELS):
        m = c >> (l + 1)
        for r in range(c):
            n = (r // (2 * m)) * 2 * m + m - 1
            lo, hi = (r, n) if r <= n else (n, r)
            w[l, r, lo + 1:hi + 1] = 1.0
        for t in range(c):
            for s in range(c):
                if t // (2 * m) == s // (2 * m) and (t // m) % 2 == 1 and (s // m) % 2 == 0:
                    level[t, s] = l
    for r in range(c):
        w[N_LEVELS, r, :r + 1] = 1.0
        w[N_LEVELS + 1, r, r + 1:] = 1.0
        level[r, r] = N_LEVELS
    w_f = w.reshape(-1, c)
    w_b = w[:, ::-1, ::-1].reshape(-1, c)
    return (jnp.asarray(w_f, BF16), jnp.asarray(w_b, BF16),
            jnp.asarray(level), jnp.asarray(level[::-1, ::-1].copy()))


def _scan_body(wf_ref, wb_ref, lvf_ref, lvb_ref,
               qf_ref, vf_ref, kf_ref, hf_ref, lf_ref,
               qb_ref, vb_ref, kb_ref, hb_ref, lb_ref,
               of_ref, ob_ref, s_ref, *, n_heads):
    tb, d = qf_ref.shape
    n_chunks = tb // CHUNK
    c = CHUNK

    @pl.when(pl.program_id(0) == 0)
    def _():
        s_ref[...] = jnp.zeros_like(s_ref)

    dirs = (
        (0, wf_ref, lvf_ref, qf_ref, vf_ref, kf_ref, hf_ref, lf_ref, of_ref, c - 1),
        (1, wb_ref, lvb_ref, qb_ref, vb_ref, kb_ref, hb_ref, lb_ref, ob_ref, 0),
    )
    masks = []
    for _, _, lv_ref, *_ in dirs:
        lv = lv_ref[...]
        masks.append([lv == l for l in range(N_LEVELS + 1)])
    contract_last = (((1,), (1,)), ((), ()))
    contract_rows = (((0,), (0,)), ((), ()))

    def chunk_step(ci, carry):
        for (di, w_ref, _, q_ref, v_ref, k_ref, hi_ref, lo_ref, o_ref, total_row) in dirs:
            cidx = ci if di == 0 else n_chunks - 1 - ci
            r0 = pl.multiple_of(cidx * c, c)
            rows = pl.ds(r0, c)
            q, k, v = q_ref[rows, :], k_ref[rows, :], v_ref[rows, :]
            w = w_ref[...]
            x = (jnp.dot(w, hi_ref[rows, :], preferred_element_type=F32)
                 + jnp.dot(w, lo_ref[rows, :], preferred_element_type=F32))
            e = jnp.exp(x)
            e16 = e.astype(BF16)
            qt = [q * e16[l * c:(l + 1) * c] for l in range(N_LEVELS)] + [q]
            kt = [k * e16[l * c:(l + 1) * c] for l in range(N_LEVELS)] + [k]
            q_in = q * e16[N_LEVELS * c:(N_LEVELS + 1) * c]
            k_out = k * e16[(N_LEVELS + 1) * c:(N_LEVELS + 2) * c]
            decay = e[N_LEVELS * c + total_row:N_LEVELS * c + total_row + 1, :]
            for h in range(n_heads):
                sl = slice(h * HEAD_DIM, (h + 1) * HEAD_DIM)
                scores = jnp.zeros((c, c), F32)
                for l in range(N_LEVELS + 1):
                    a = lax.dot_general(qt[l][:, sl], kt[l][:, sl], contract_last,
                                        preferred_element_type=F32)
                    scores = jnp.where(masks[di][l], a, scores)
                intra = jnp.dot(scores.astype(BF16), v[:, sl], preferred_element_type=F32)
                st = s_ref[di * n_heads + h]
                inter = lax.dot_general(q_in[:, sl], st.astype(BF16), contract_last,
                                        preferred_element_type=F32)
                o_ref[rows, sl] = inter + intra
                ds = lax.dot_general(v[:, sl], k_out[:, sl], contract_rows,
                                     preferred_element_type=F32)
                s_ref[di * n_heads + h] = st * decay[:, sl] + ds
        return carry

    lax.fori_loop(0, n_chunks, chunk_step, 0)


def _scan(q, v, kf, lfh, lfl, kb, lbh, lbl):
    nt, d = q.shape
    tb = _row_tile(nt)
    nb = nt // tb
    n_heads = d // HEAD_DIM
    w_f, w_b, lv_f, lv_b = _scan_tables()
    fwd = pl.BlockSpec((tb, d), lambda i: (i, 0))
    bwd = pl.BlockSpec((tb, d), lambda i: (nb - 1 - i, 0))
    return pl.pallas_call(
        functools.partial(_scan_body, n_heads=n_heads),
        grid=(nb,),
        in_specs=[_resident(w_f.shape), _resident(w_b.shape), _resident(lv_f.shape),
                  _resident(lv_b.shape)] + [fwd] * 5 + [bwd] * 5,
        out_specs=[fwd, bwd],
        out_shape=[jax.ShapeDtypeStruct((nt, d), F32)] * 2,
        scratch_shapes=[pltpu.VMEM((2 * n_heads, HEAD_DIM, HEAD_DIM), F32)],
        compiler_params=_params(),
        name="hg_scan",
    )(w_f, w_b, lv_f, lv_b, q, v, kf, lfh, lfl, q, v, kb, lbh, lbl)


def _hg_out_body(of_ref, ob_ref, g_ref, h_ref, gn_ref, wout_ref, gpost_ref, o_ref):
    d = h_ref.shape[1]
    gated = (of_ref[...] + ob_ref[...]) * g_ref[...].astype(F32)
    parts = []
    for h in range(d // HEAD_DIM):
        gh = gated[:, h * HEAD_DIM:(h + 1) * HEAD_DIM]
        parts.append(gh * lax.rsqrt(jnp.mean(gh * gh, axis=-1, keepdims=True) + EPS))
    normed = jnp.concatenate(parts, axis=-1) * gn_ref[...]
    y = jnp.dot(normed.astype(BF16), wout_ref[...], preferred_element_type=F32)
    o_ref[...] = h_ref[...] + _rms(y, gpost_ref[...])


def _hg_out(o_f, o_b, g, h, gn, w_out, gpost):
    nt, d = h.shape
    tm = _row_tile(nt)
    row = pl.BlockSpec((tm, d), lambda i: (i, 0))
    vec = _resident((1, d))
    return pl.pallas_call(
        _hg_out_body,
        grid=(nt // tm,),
        in_specs=[row, row, row, row, vec, _resident((d, d)), vec],
        out_specs=row,
        out_shape=jax.ShapeDtypeStruct((nt, d), F32),
        compiler_params=_params(),
        name="hg_out",
    )(o_f, o_b, g, h, gn, w_out, gpost)


def _boundary_tables(seq_lens, n_rows, tile):
    n_tiles = n_rows // tile
    start = np.full((n_tiles,), -1, np.int32)
    end = np.full((n_tiles,), -1, np.int32)
    base = 0
    for length in seq_lens:
        assert length >= 2 * tile, "at most one sequence boundary per row tile"
        start[base // tile] = base % tile
        last = base + length - 1
        end[last // tile] = last % tile
        base += length
    return jnp.asarray(start), jnp.asarray(end)


def kernel(x_prompt, x_sample, meta_tokens, norm_pre, norm_post, ffn_w_gate, ffn_w_up, ffn_w_down,
           sc_w_in, sc_conv, sc_w_out, hg_w_in, hg_lb_logits, hg_gn, hg_w_out, final_norm):
    d = x_prompt.shape[-1]
    depth = norm_pre.shape[0]
    assert d % HEAD_DIM == 0

    pieces, seq_lens = [], []
    for x in (x_prompt, x_sample):
        bn, s, _ = x.shape
        meta = jnp.broadcast_to(meta_tokens.astype(x.dtype)[None], (bn, N_META, d))
        pieces.append(jnp.concatenate([meta, x], axis=1).reshape(bn * (N_META + s), d))
        seq_lens += [N_META + s] * bn
    n_real = sum(seq_lens)
    tile = _row_tile(n_real)
    n_rows = -(-n_real // tile) * tile
    if n_rows > n_real:
        pieces.append(jnp.zeros((n_rows - n_real, d), x_prompt.dtype))
    h = jnp.concatenate(pieces, axis=0)
    start_off, end_off = _boundary_tables(seq_lens, n_rows, tile)

    vec = lambda g: g.reshape(1, d).astype(F32)
    w16 = lambda w: w.astype(BF16)
    logits = hg_lb_logits.reshape(2 * depth, -1).astype(F32)

    for i in range(depth):
        last = i == depth - 1
        h = _ffn(h, vec(norm_pre[i, 0]), vec(norm_post[i, 0]),
                 w16(ffn_w_gate[i, 0]), w16(ffn_w_up[i, 0]), w16(ffn_w_down[i, 0]))
        j = i // 2
        if i % 2 == 0:
            gb, z = _conv_in(h, vec(norm_pre[i, 1]), w16(sc_w_in[j]))
            h = _conv_out(z, gb, h, sc_conv[j].astype(F32), w16(sc_w_out[j]), vec(norm_post[i, 1]),
                          start_off, end_off)
        else:
            q, v, g, kf, kb, lfh, lfl, lbh, lbl = _hg_in(
                h, vec(norm_pre[i, 1]), w16(hg_w_in[j]), logits, start_off, end_off, layer=i)
            o_f, o_b = _scan(q, v, kf, lfh, lfl, kb, lbh, lbl)
            h = _hg_out(o_f, o_b, g, h, vec(hg_gn[j]), w16(hg_w_out[j]), vec(norm_post[i, 1]))
        h = _ffn(h, vec(norm_pre[i, 2]), vec(norm_post[i, 2]),
                 w16(ffn_w_gate[i, 1]), w16(ffn_w_up[i, 1]), w16(ffn_w_down[i, 1]),
                 gfinal=vec(final_norm) if last else None)

    outs, base = [], 0
    for x in (x_prompt, x_sample):
        bn, s, _ = x.shape
        rows = bn * (N_META + s)
        outs.append(h[base:base + rows].reshape(bn, N_META + s, d)[:, N_META:])
        base += rows
    return tuple(outs)
```

```python
import functools

import numpy as np
import jax
import jax.numpy as jnp
from jax import lax
from jax.experimental import pallas as pl
from jax.experimental.pallas import tpu as pltpu

EPS = 1e-6
N_META = 16
HEAD_DIM = 128
CHUNK = 64
N_LEVELS = 6
N_VPU_LEVELS = 4
NEG_BIG = -1000.0
LOG2E = 1.4426950408889634
MXU_TILE = 256
VMEM_LIMIT_BYTES = 56 * 1024 * 1024

F32 = jnp.float32
BF16 = jnp.bfloat16


def _rms(x, g):
    return x * lax.rsqrt(jnp.mean(x * x, axis=-1, keepdims=True) + EPS) * g


def _exp_neg(x):
    return jnp.exp2(x * (-LOG2E))


def _silu(x):
    return x / (1.0 + _exp_neg(x))


def _row_tile(n_rows):
    return 512 if n_rows >= 4096 else 128


def _resident(shape):
    nd = len(shape)
    return pl.BlockSpec(shape, lambda i, *_: (0,) * nd, pipeline_mode=pl.Buffered(1))


def _params():
    return pltpu.CompilerParams(dimension_semantics=("arbitrary",),
                                vmem_limit_bytes=VMEM_LIMIT_BYTES)


def _ffn_splits(d_ff):
    if d_ff % MXU_TILE or d_ff < 4 * MXU_TILE:
        return ((0, d_ff),)
    half = (d_ff // MXU_TILE + 1) // 2 * MXU_TILE
    return ((0, half), (half, d_ff))


def _ffn_body(x_ref, gpre_ref, gpost_ref, wg_ref, wu_ref, wd_ref, *rest, final):
    o_ref = rest[-1]
    x = x_ref[...]
    u = _rms(x, gpre_ref[...]).astype(BF16)
    f = None
    for lo, hi in _ffn_splits(wg_ref.shape[1]):
        a = jnp.dot(u, wg_ref[:, lo:hi], preferred_element_type=F32)
        b = jnp.dot(u, wu_ref[:, lo:hi], preferred_element_type=F32)
        hdn = (_silu(a) * b).astype(BF16)
        part = jnp.dot(hdn, wd_ref[lo:hi, :], preferred_element_type=F32)
        f = part if f is None else f + part
    y = x + 0.5 * _rms(f, gpost_ref[...])
    if final:
        y = _rms(y, rest[0][...])
    o_ref[...] = y


def _ffn(x, gpre, gpost, wg, wu, wd, gfinal=None):
    nt, d = x.shape
    d_ff = wg.shape[1]
    tm = _row_tile(nt)
    row = pl.BlockSpec((tm, d), lambda i: (i, 0))
    vec = _resident((1, d))
    in_specs = [row, vec, vec, _resident((d, d_ff)), _resident((d, d_ff)), _resident((d_ff, d))]
    args = [x, gpre, gpost, wg, wu, wd]
    if gfinal is not None:
        in_specs.append(vec)
        args.append(gfinal)
    return pl.pallas_call(
        functools.partial(_ffn_body, final=gfinal is not None),
        grid=(nt // tm,),
        in_specs=in_specs,
        out_specs=row,
        out_shape=jax.ShapeDtypeStruct((nt, d), F32),
        compiler_params=_params(),
        name="ffn",
    )(*args)


def _conv_in_body(x_ref, gpre_ref, win_ref, gb_ref, z_ref):
    d = x_ref.shape[1]
    u = _rms(x_ref[...], gpre_ref[...]).astype(BF16)
    p = jnp.dot(u, win_ref[...], preferred_element_type=F32)
    gb_ref[...] = p[:, :d]
    z_ref[...] = p[:, d:2 * d] * p[:, 2 * d:]


def _conv_in(x, gpre, w_in):
    nt, d = x.shape
    tm = _row_tile(nt)
    row = pl.BlockSpec((tm, d), lambda i: (i, 0))
    return pl.pallas_call(
        _conv_in_body,
        grid=(nt // tm,),
        in_specs=[row, _resident((1, d)), _resident((d, 3 * d))],
        out_specs=[row, row],
        out_shape=[jax.ShapeDtypeStruct((nt, d), F32)] * 2,
        compiler_params=_params(),
        name="conv_in",
    )(x, gpre, w_in)


def _conv_out_body(so_ref, eo_ref, z_ref, zp_ref, zn_ref, gb_ref, h_ref, wc_ref, wout_ref,
                   gpost_ref, o_ref):
    i = pl.program_id(0)
    tm = z_ref.shape[0]
    z = z_ref[...]
    row = lax.broadcasted_iota(jnp.int32, (tm, 1), 0)
    zm1 = jnp.where(row == 0, zp_ref[7:8, :], pltpu.roll(z, 1, 0))
    zm1 = jnp.where(row == so_ref[i], 0.0, zm1)
    zp1 = jnp.where(row == tm - 1, zn_ref[0:1, :], pltpu.roll(z, tm - 1, 0))
    zp1 = jnp.where(row == eo_ref[i], 0.0, zp1)
    conv = wc_ref[0:1, :] * zm1 + wc_ref[1:2, :] * z + wc_ref[2:3, :] * zp1
    y = jnp.dot((gb_ref[...] * conv).astype(BF16), wout_ref[...], preferred_element_type=F32)
    o_ref[...] = h_ref[...] + _rms(y, gpost_ref[...])


def _conv_out(z, gb, h, w_conv, w_out, gpost, start_off, end_off):
    nt, d = z.shape
    tm = _row_tile(nt)
    sub = tm // 8
    n_sub = nt // 8
    row = pl.BlockSpec((tm, d), lambda i, *_: (i, 0))
    prev8 = pl.BlockSpec((8, d), lambda i, *_: (jnp.maximum(i * sub - 1, 0), 0))
    next8 = pl.BlockSpec((8, d), lambda i, *_: (jnp.minimum((i + 1) * sub, n_sub - 1), 0))
    grid_spec = pltpu.PrefetchScalarGridSpec(
        num_scalar_prefetch=2,
        grid=(nt // tm,),
        in_specs=[row, prev8, next8, row, row, _resident(w_conv.shape), _resident((d, d)),
                  _resident((1, d))],
        out_specs=row,
    )
    return pl.pallas_call(
        _conv_out_body,
        grid_spec=grid_spec,
        out_shape=jax.ShapeDtypeStruct((nt, d), F32),
        compiler_params=_params(),
        name="conv_out",
    )(start_off, end_off, z, z, z, gb, h, w_conv, w_out, gpost)


def _log_gates(z, lb):
    log_sig = jnp.minimum(z, 0.0) - jnp.log(1.0 + _exp_neg(jnp.abs(z)))
    a = jnp.log(lb)
    b = jnp.log1p(-lb) + log_sig
    logf = jnp.maximum(a, b) + jnp.log(1.0 + _exp_neg(jnp.abs(a - b)))
    key = (1.0 - lb) * _exp_neg(z - log_sig)
    return logf, key


def _split_hi_lo(x):
    hi = x.astype(BF16)
    lo = (x - hi.astype(F32)).astype(BF16)
    return hi, lo


def _hg_in_body(so_ref, eo_ref, x_ref, gpre_ref, win_ref, logit_ref,
                q_ref, v_ref, g_ref, kf_ref, kb_ref, hlf_ref, hlb_ref, *, layer):
    i = pl.program_id(0)
    tm, d = x_ref.shape
    u = _rms(x_ref[...], gpre_ref[...]).astype(BF16)
    depth = logit_ref.shape[0] // 2
    row = lax.broadcasted_iota(jnp.int32, (tm, 1), 0)

    def proj(j, cols):
        return jnp.dot(u, win_ref[:, j * d + cols.start:j * d + cols.stop],
                       preferred_element_type=F32)

    for cols in (slice(j, min(j + MXU_TILE, d)) for j in range(0, d, MXU_TILE)):
        q_ref[:, cols] = _silu(proj(0, cols)).astype(BF16)
        v_ref[:, cols] = proj(1, cols).astype(BF16)
        g_ref[:, cols] = _silu(proj(4, cols)).astype(BF16)
        for direction, (k_ref, hl_ref, off_ref) in enumerate(
                ((kf_ref, hlf_ref, so_ref), (kb_ref, hlb_ref, eo_ref))):
            logits = [logit_ref[2 * j + direction:2 * j + direction + 1, cols]
                      for j in range(depth)]
            m = functools.reduce(jnp.maximum, logits)
            e = [jnp.exp(l - m) for l in logits]
            lb = sum(e[1:layer + 1]) / sum(e)
            logf, key = _log_gates(proj(2 + direction, cols), lb)
            logf = jnp.where(row == off_ref[i], NEG_BIG, logf)
            k_ref[:, cols] = key.astype(BF16)
            hi, lo = _split_hi_lo(logf)
            for c in range(tm // CHUNK):
                hl_ref[2 * c * CHUNK:(2 * c + 1) * CHUNK, cols] = hi[c * CHUNK:(c + 1) * CHUNK]
                hl_ref[(2 * c + 1) * CHUNK:(2 * c + 2) * CHUNK, cols] = lo[c * CHUNK:(c + 1) * CHUNK]


def _hg_in(x, gpre, w_in, logits, start_off, end_off, layer):
    nt, d = x.shape
    tm = _row_tile(nt)
    row = pl.BlockSpec((tm, d), lambda i, *_: (i, 0))
    row2 = pl.BlockSpec((2 * tm, d), lambda i, *_: (i, 0))
    grid_spec = pltpu.PrefetchScalarGridSpec(
        num_scalar_prefetch=2,
        grid=(nt // tm,),
        in_specs=[row, _resident((1, d)), _resident((d, 5 * d)), _resident(logits.shape)],
        out_specs=[row] * 5 + [row2] * 2,
    )
    return pl.pallas_call(
        functools.partial(_hg_in_body, layer=layer),
        grid_spec=grid_spec,
        out_shape=[jax.ShapeDtypeStruct((nt, d), BF16)] * 5
        + [jax.ShapeDtypeStruct((2 * nt, d), BF16)] * 2,
        compiler_params=_params(),
        name="hg_in",
    )(start_off, end_off, x, gpre, w_in, logits)


def _scan_tables():
    c = CHUNK
    w = np.zeros((N_LEVELS + 2, c, c), np.float32)
    level = np.full((c, c), N_LEVELS + 1, np.int32)
    for l in range(N_LEVELS):
        m = c >> (l + 1)
        for r in range(c):
            n = (r // (2 * m)) * 2 * m + m - 1
            lo, hi = (r, n) if r <= n else (n, r)
            w[l, r, lo + 1:hi + 1] = 1.0
        for t in range(c):
            for s in range(c):
                if t // (2 * m) == s // (2 * m) and (t // m) % 2 == 1 and (s // m) % 2 == 0:
                    level[t, s] = l
    for r in range(c):
        w[N_LEVELS, r, :r + 1] = 1.0
        w[N_LEVELS + 1, r, r + 1:] = 1.0
        level[r, r] = N_LEVELS
    keep = [N_LEVELS] + list(range(N_VPU_LEVELS, N_LEVELS))
    w_f = np.tile(w[keep].reshape(-1, c), (1, 2))
    w_b = np.tile(w[keep][:, ::-1, ::-1].reshape(-1, c), (1, 2))
    return (jnp.asarray(w_f, BF16), jnp.asarray(w_b, BF16),
            jnp.asarray(level), jnp.asarray(level[::-1, ::-1].copy()))


def _reference_rows(level, backward):
    c = CHUNK
    if level == N_LEVELS + 1:
        return [(0, c, 0 if backward else c - 1)]
    m = c >> (level + 1)
    return [(g, g + 2 * m, g + (m if backward else m - 1)) for g in range(0, c, 2 * m)]


def _scan_body(wf_ref, wb_ref, lvf_ref, lvb_ref,
               qf_ref, vf_ref, kf_ref, hlf_ref,
               qb_ref, vb_ref, kb_ref, hlb_ref,
               of_ref, ob_ref, s_ref, e_ref, dec_ref, *, n_heads):
    tb, d = qf_ref.shape
    n_chunks = tb // CHUNK
    c = CHUNK

    @pl.when(pl.program_id(0) == 0)
    def _():
        s_ref[...] = jnp.zeros_like(s_ref)

    dirs = (
        (0, wf_ref, lvf_ref, qf_ref, vf_ref, kf_ref, hlf_ref, of_ref, c - 1),
        (1, wb_ref, lvb_ref, qb_ref, vb_ref, kb_ref, hlb_ref, ob_ref, 0),
    )
    masks = []
    for _, _, lv_ref, *_ in dirs:
        lv = lv_ref[...]
        masks.append([lv == l for l in range(N_LEVELS + 1)])
    contract_last = (((1,), (1,)), ((), ()))
    contract_rows = (((0,), (0,)), ((), ()))
    heads = [slice(h * HEAD_DIM, (h + 1) * HEAD_DIM) for h in range(n_heads)]
    lane_blocks = [slice(j, min(j + MXU_TILE, d)) for j in range(0, d, MXU_TILE)]

    def chunk_index(di, ci):
        return ci if di == 0 else n_chunks - 1 - ci

    def decay_factors(buf, ci):
        for di, w_ref, _, _, _, _, hl_ref, _, total_row in dirs:
            start = pl.multiple_of(chunk_index(di, ci) * 2 * c, 2 * c)
            for lanes in lane_blocks:
                x = jnp.dot(w_ref[...], hl_ref[pl.ds(start, 2 * c), lanes],
                            preferred_element_type=F32)
                b = x[0:c]
                e_b = jnp.exp(b)
                e_ref[buf, di, N_LEVELS * c:(N_LEVELS + 1) * c, lanes] = e_b.astype(BF16)
                dec_ref[buf, di, :, lanes] = e_b[total_row:total_row + 1, :]
                e_ref[buf, di, N_VPU_LEVELS * c:N_LEVELS * c, lanes] = jnp.exp(x[c:]).astype(BF16)
                for l in list(range(N_VPU_LEVELS)) + [N_LEVELS + 1]:
                    ref = jnp.concatenate(
                        [jnp.broadcast_to(b[n:n + 1, :], (hi - lo, b.shape[1]))
                         for lo, hi, n in _reference_rows(l, di == 1)], axis=0)
                    e_ref[buf, di, l * c:(l + 1) * c, lanes] = (
                        _exp_neg(jnp.abs(b - ref)).astype(BF16))

    def recurrence(buf, ci):
        def factor(di, group, sl):
            return e_ref[buf, di, group * c:(group + 1) * c, sl]

        inter, scores = {}, {}
        rows = [pl.ds(pl.multiple_of(chunk_index(di, ci) * c, c), c) for di in range(2)]
        for di, _, _, q_ref, v_ref, k_ref, _, _, _ in dirs:
            for h, sl in enumerate(heads):
                q, k = q_ref[rows[di], sl], k_ref[rows[di], sl]
                q_in = q * factor(di, N_LEVELS, sl)
                k_out = k * factor(di, N_LEVELS + 1, sl)
                st = s_ref[di * n_heads + h]
                inter[di, h] = lax.dot_general(q_in, st.astype(BF16), contract_last,
                                               preferred_element_type=F32)
                ds = lax.dot_general(v_ref[rows[di], sl], k_out, contract_rows,
                                     preferred_element_type=F32)
                s_ref[di * n_heads + h] = st * dec_ref[buf, di, :, sl] + ds
                sc = jnp.zeros((c, c), F32)
                for l in range(N_LEVELS + 1):
                    if l < N_LEVELS:
                        f = factor(di, l, sl)
                        a = lax.dot_general(q * f, k * f, contract_last,
                                            preferred_element_type=F32)
                    else:
                        a = lax.dot_general(q, k, contract_last, preferred_element_type=F32)
                    sc = jnp.where(masks[di][l], a, sc)
                scores[di, h] = sc.astype(BF16)
        for di, _, _, _, v_ref, _, _, o_ref, _ in dirs:
            for h, sl in enumerate(heads):
                intra = jnp.dot(scores[di, h], v_ref[rows[di], sl], preferred_element_type=F32)
                o_ref[rows[di], sl] = inter[di, h] + intra

    decay_factors(0, 0)

    def step_pair(j, carry):
        for half in range(2):
            ci = 2 * j + half
            decay_factors(1 - half, jnp.minimum(ci + 1, n_chunks - 1))
            recurrence(half, ci)
        return carry

    assert n_chunks % 2 == 0
    lax.fori_loop(0, n_chunks // 2, step_pair, 0)


def _scan(q, v, kf, hlf, kb, hlb):
    nt, d = q.shape
    tb = _row_tile(nt)
    nb = nt // tb
    n_heads = d // HEAD_DIM
    w_f, w_b, lv_f, lv_b = _scan_tables()
    fwd = pl.BlockSpec((tb, d), lambda i: (i, 0))
    bwd = pl.BlockSpec((tb, d), lambda i: (nb - 1 - i, 0))
    fwd2 = pl.BlockSpec((2 * tb, d), lambda i: (i, 0))
    bwd2 = pl.BlockSpec((2 * tb, d), lambda i: (nb - 1 - i, 0))
    return pl.pallas_call(
        functools.partial(_scan_body, n_heads=n_heads),
        grid=(nb,),
        in_specs=[_resident(w_f.shape), _resident(w_b.shape), _resident(lv_f.shape),
                  _resident(lv_b.shape), fwd, fwd, fwd, fwd2, bwd, bwd, bwd, bwd2],
        out_specs=[fwd, bwd],
        out_shape=[jax.ShapeDtypeStruct((nt, d), F32)] * 2,
        scratch_shapes=[
            pltpu.VMEM((2 * n_heads, HEAD_DIM, HEAD_DIM), F32),
            pltpu.VMEM((2, 2, (N_LEVELS + 2) * CHUNK, d), BF16),
            pltpu.VMEM((2, 2, 1, d), F32),
        ],
        compiler_params=_params(),
        name="hg_scan",
    )(w_f, w_b, lv_f, lv_b, q, v, kf, hlf, q, v, kb, hlb)


def _hg_out_body(of_ref, ob_ref, g_ref, h_ref, gn_ref, wout_ref, gpost_ref, o_ref):
    d = h_ref.shape[1]
    gated = (of_ref[...] + ob_ref[...]) * g_ref[...].astype(F32)
    parts = []
    for h in range(d // HEAD_DIM):
        gh = gated[:, h * HEAD_DIM:(h + 1) * HEAD_DIM]
        parts.append(gh * lax.rsqrt(jnp.mean(gh * gh, axis=-1, keepdims=True) + EPS))
    normed = jnp.concatenate(parts, axis=-1) * gn_ref[...]
    y = jnp.dot(normed.astype(BF16), wout_ref[...], preferred_element_type=F32)
    o_ref[...] = h_ref[...] + _rms(y, gpost_ref[...])


def _hg_out(o_f, o_b, g, h, gn, w_out, gpost):
    nt, d = h.shape
    tm = _row_tile(nt)
    row = pl.BlockSpec((tm, d), lambda i: (i, 0))
    vec = _resident((1, d))
    return pl.pallas_call(
        _hg_out_body,
        grid=(nt // tm,),
        in_specs=[row, row, row, row, vec, _resident((d, d)), vec],
        out_specs=row,
        out_shape=jax.ShapeDtypeStruct((nt, d), F32),
        compiler_params=_params(),
        name="hg_out",
    )(o_f, o_b, g, h, gn, w_out, gpost)


def _boundary_tables(seq_lens, n_rows, tile):
    n_tiles = n_rows // tile
    start = np.full((n_tiles,), -1, np.int32)
    end = np.full((n_tiles,), -1, np.int32)
    base = 0
    for length in seq_lens:
        assert length >= 2 * tile, "at most one sequence boundary per row tile"
        start[base // tile] = base % tile
        last = base + length - 1
        end[last // tile] = last % tile
        base += length
    return jnp.asarray(start), jnp.asarray(end)


def kernel(x_prompt, x_sample, meta_tokens, norm_pre, norm_post, ffn_w_gate, ffn_w_up, ffn_w_down,
           sc_w_in, sc_conv, sc_w_out, hg_w_in, hg_lb_logits, hg_gn, hg_w_out, final_norm):
    d = x_prompt.shape[-1]
    depth = norm_pre.shape[0]
    assert d % HEAD_DIM == 0

    pieces, seq_lens = [], []
    for x in (x_prompt, x_sample):
        bn, s, _ = x.shape
        meta = jnp.broadcast_to(meta_tokens.astype(x.dtype)[None], (bn, N_META, d))
        pieces.append(jnp.concatenate([meta, x], axis=1).reshape(bn * (N_META + s), d))
        seq_lens += [N_META + s] * bn
    n_real = sum(seq_lens)
    tile = _row_tile(n_real)
    n_rows = -(-n_real // tile) * tile
    if n_rows > n_real:
        pieces.append(jnp.zeros((n_rows - n_real, d), x_prompt.dtype))
    h = jnp.concatenate(pieces, axis=0)
    start_off, end_off = _boundary_tables(seq_lens, n_rows, tile)

    vec = lambda g: g.reshape(1, d).astype(F32)
    w16 = lambda w: w.astype(BF16)
    logits = hg_lb_logits.reshape(2 * depth, -1).astype(F32)

    for i in range(depth):
        last = i == depth - 1
        h = _ffn(h, vec(norm_pre[i, 0]), vec(norm_post[i, 0]),
                 w16(ffn_w_gate[i, 0]), w16(ffn_w_up[i, 0]), w16(ffn_w_down[i, 0]))
        j = i // 2
        if i % 2 == 0:
            gb, z = _conv_in(h, vec(norm_pre[i, 1]), w16(sc_w_in[j]))
            h = _conv_out(z, gb, h, sc_conv[j].astype(F32), w16(sc_w_out[j]), vec(norm_post[i, 1]),
                          start_off, end_off)
        else:
            q, v, g, kf, kb, hlf, hlb = _hg_in(
                h, vec(norm_pre[i, 1]), w16(hg_w_in[j]), logits, start_off, end_off, layer=i)
            o_f, o_b = _scan(q, v, kf, hlf, kb, hlb)
            h = _hg_out(o_f, o_b, g, h, vec(hg_gn[j]), w16(hg_w_out[j]), vec(norm_post[i, 1]))
        h = _ffn(h, vec(norm_pre[i, 2]), vec(norm_post[i, 2]),
                 w16(ffn_w_gate[i, 1]), w16(ffn_w_up[i, 1]), w16(ffn_w_down[i, 1]),
                 gfinal=vec(final_norm) if last else None)

    outs, base = [], 0
    for x in (x_prompt, x_sample):
        bn, s, _ = x.shape
        rows = bn * (N_META + s)
        outs.append(h[base:base + rows].reshape(bn, N_META + s, d)[:, N_META:])
        base += rows
    return tuple(outs)
```

```python
import functools

import numpy as np
import jax
import jax.numpy as jnp
from jax import lax
from jax.experimental import pallas as pl
from jax.experimental.pallas import tpu as pltpu

EPS = 1e-6
N_META = 16
HEAD_DIM = 128
CHUNK = 64
N_LEVELS = 6
N_VPU_LEVELS = 4
NEG_BIG = -1000.0
LOG2E = 1.4426950408889634
MXU_TILE = 256
VMEM_LIMIT_BYTES = 56 * 1024 * 1024

F32 = jnp.float32
BF16 = jnp.bfloat16


def _rms(x, g):
    return x * lax.rsqrt(jnp.mean(x * x, axis=-1, keepdims=True) + EPS) * g


def _exp_neg(x):
    return jnp.exp2(x * (-LOG2E))


def _silu(x):
    return x / (1.0 + _exp_neg(x))


def _row_tile(n_rows):
    return 512 if n_rows >= 4096 else 128


def _resident(shape):
    nd = len(shape)
    return pl.BlockSpec(shape, lambda i, *_: (0,) * nd, pipeline_mode=pl.Buffered(1))


def _params():
    return pltpu.CompilerParams(dimension_semantics=("arbitrary",),
                                vmem_limit_bytes=VMEM_LIMIT_BYTES)


def _ffn_splits(d_ff):
    if d_ff % MXU_TILE or d_ff < 4 * MXU_TILE:
        return ((0, d_ff),)
    half = (d_ff // MXU_TILE + 1) // 2 * MXU_TILE
    return ((0, half), (half, d_ff))


def _ffn_math(x, gpre_ref, gpost_ref, wg_ref, wu_ref, wd_ref):
    u = _rms(x, gpre_ref[...]).astype(BF16)
    f = None
    for lo, hi in _ffn_splits(wg_ref.shape[1]):
        a = jnp.dot(u, wg_ref[:, lo:hi], preferred_element_type=F32)
        b = jnp.dot(u, wu_ref[:, lo:hi], preferred_element_type=F32)
        hdn = (_silu(a) * b).astype(BF16)
        part = jnp.dot(hdn, wd_ref[lo:hi, :], preferred_element_type=F32)
        f = part if f is None else f + part
    return x + 0.5 * _rms(f, gpost_ref[...])


def _ffn_weight_specs(d, d_ff):
    vec = _resident((1, d))
    return [vec, vec, _resident((d, d_ff)), _resident((d, d_ff)), _resident((d_ff, d))]


def _ffn_body(x_ref, gpre_ref, gpost_ref, wg_ref, wu_ref, wd_ref, o_ref):
    o_ref[...] = _ffn_math(x_ref[...], gpre_ref, gpost_ref, wg_ref, wu_ref, wd_ref)


def _ffn(x, gpre, gpost, wg, wu, wd):
    nt, d = x.shape
    tm = _row_tile(nt)
    row = pl.BlockSpec((tm, d), lambda i: (i, 0))
    return pl.pallas_call(
        _ffn_body,
        grid=(nt // tm,),
        in_specs=[row] + _ffn_weight_specs(d, wg.shape[1]),
        out_specs=row,
        out_shape=jax.ShapeDtypeStruct((nt, d), F32),
        compiler_params=_params(),
        name="ffn",
    )(x, gpre, gpost, wg, wu, wd)


class _Layout:
    def __init__(self, shapes):
        self.arr, self.base, self.first, self.length = [], [], [], []
        self.real = [bn * s for bn, s, _ in shapes]
        row = 0
        for a, (bn, s, _) in enumerate(shapes):
            for b in range(bn):
                self.arr.append(a)
                self.base.append(row)
                self.first.append(b * s)
                self.length.append(N_META + s)
                row += N_META + s
        self.n_real = row
        self.tile = _row_tile(row)
        self.n_rows = -(-row // self.tile) * self.tile
        t = self.tile
        for a, (bn, s, _) in enumerate(shapes):
            assert s % t == 0 and s % 16 == 0 and N_META + s >= 2 * t, (s, t)

    def seq_of(self, row):
        return max(s for s in range(len(self.base)) if self.base[s] <= row)

    def gather_tables(self):
        t = self.tile
        n_tiles = self.n_rows // t
        cols = {k: np.zeros((n_tiles,), np.int32)
                for k in ("blk0", "blk1", "src1", "e1", "src2", "e2", "off", "valid")}
        held = [0, 0]
        for i in range(n_tiles):
            u0 = i * t
            starts = [s for s in range(len(self.base)) if u0 <= self.base[s] < u0 + t]
            assert len(starts) <= 1
            off = self.base[starts[0]] - u0 if starts else t
            windows = []
            if off > 0 and u0 < self.n_real:
                windows.append((1, self.seq_of(u0)))
            if starts:
                windows.append((2, starts[0]))
            pairs = {}
            for which, s in windows:
                w = u0 - self.base[s] - N_META + self.first[s]
                blk, e = w // t, w % t
                assert pairs.setdefault(self.arr[s], blk) == blk and e % 16 == 0
                cols["src%d" % which][i], cols["e%d" % which][i] = self.arr[s], e
            for a in range(2):
                held[a] = pairs.get(a, held[a])
                cols["blk%d" % a][i] = held[a]
            cols["off"][i] = off
            cols["valid"][i] = min(max(self.n_real - u0, 0), t)
        return [jnp.asarray(cols[k]) for k in
                ("blk0", "blk1", "src1", "e1", "src2", "e2", "off", "valid")]

    def scatter_tables(self):
        t = self.tile
        blk, e = [], []
        for s in range(len(self.base)):
            for j in range((self.length[s] - N_META) // t):
                r = self.base[s] + N_META + j * t
                blk.append(r // t)
                e.append(r % t)
                assert r % 16 == 0
        return jnp.asarray(blk, jnp.int32), jnp.asarray(e, jnp.int32)


def _ffn_first_body(b0_ref, b1_ref, s1_ref, e1_ref, s2_ref, e2_ref, off_ref, valid_ref,
                    pa_ref, pb_ref, sa_ref, sb_ref, meta_ref,
                    gpre_ref, gpost_ref, wg_ref, wu_ref, wd_ref, o_ref, pair_ref, x_ref):
    i = pl.program_id(0)
    t = o_ref.shape[0]
    pair_ref[0, 0:t, :] = pa_ref[...]
    pair_ref[0, t:2 * t, :] = pb_ref[...]
    pair_ref[1, 0:t, :] = sa_ref[...]
    pair_ref[1, t:2 * t, :] = sb_ref[...]

    def window(src, e):
        return pair_ref[src, pl.ds(pl.multiple_of(e, 16), t), :]

    row = lax.broadcasted_iota(jnp.int32, (t, 1), 0)
    x = jnp.where(row < off_ref[i], window(s1_ref[i], e1_ref[i]), window(s2_ref[i], e2_ref[i]))
    x_ref[...] = jnp.where(row < valid_ref[i], x, 0.0)

    @pl.when(off_ref[i] < t)
    def _():
        x_ref[pl.ds(pl.multiple_of(off_ref[i], 16), N_META), :] = meta_ref[...]

    o_ref[...] = _ffn_math(x_ref[...], gpre_ref, gpost_ref, wg_ref, wu_ref, wd_ref)


def _ffn_first(layout, x_flat, meta, gpre, gpost, wg, wu, wd):
    d = meta.shape[1]
    t = layout.tile
    tables = layout.gather_tables()
    n_blk = [x.shape[0] // t for x in x_flat]

    def block(a, second):
        def index(i, b0, b1, *_):
            return (jnp.clip((b0, b1)[a][i] + second, 0, n_blk[a] - 1), 0)
        return pl.BlockSpec((t, d), index)

    grid_spec = pltpu.PrefetchScalarGridSpec(
        num_scalar_prefetch=len(tables),
        grid=(layout.n_rows // t,),
        in_specs=[block(0, 0), block(0, 1), block(1, 0), block(1, 1), _resident((N_META, d))]
        + _ffn_weight_specs(d, wg.shape[1]),
        out_specs=pl.BlockSpec((t, d), lambda i, *_: (i, 0)),
        scratch_shapes=[pltpu.VMEM((2, 2 * t, d), F32), pltpu.VMEM((t, d), F32)],
    )
    return pl.pallas_call(
        _ffn_first_body,
        grid_spec=grid_spec,
        out_shape=jax.ShapeDtypeStruct((layout.n_rows, d), F32),
        compiler_params=_params(),
        name="ffn_first",
    )(*tables, x_flat[0], x_flat[0], x_flat[1], x_flat[1], meta, gpre, gpost, wg, wu, wd)


def _ffn_last_body(blk_ref, e_ref, ha_ref, hb_ref, gpre_ref, gpost_ref, wg_ref, wu_ref, wd_ref,
                   gfin_ref, yp_ref, ys_ref, pair_ref, *, n_first):
    i = pl.program_id(0)
    t = yp_ref.shape[0]
    pair_ref[0:t, :] = ha_ref[...]
    pair_ref[t:2 * t, :] = hb_ref[...]
    x = pair_ref[pl.ds(pl.multiple_of(e_ref[i], 16), t), :]
    y = _rms(_ffn_math(x, gpre_ref, gpost_ref, wg_ref, wu_ref, wd_ref), gfin_ref[...])

    @pl.when(i < n_first)
    def _():
        yp_ref[...] = y

    @pl.when(i >= n_first)
    def _():
        ys_ref[...] = y


def _ffn_last(layout, h, gpre, gpost, wg, wu, wd, gfinal):
    d = h.shape[1]
    t = layout.tile
    blk, e = layout.scatter_tables()
    n_out = [r // t for r in layout.real]
    n_tiles = layout.n_rows // t
    grid_spec = pltpu.PrefetchScalarGridSpec(
        num_scalar_prefetch=2,
        grid=(n_out[0] + n_out[1],),
        in_specs=[pl.BlockSpec((t, d), lambda i, blk, e: (blk[i], 0)),
                  pl.BlockSpec((t, d), lambda i, blk, e: (jnp.minimum(blk[i] + 1, n_tiles - 1), 0))]
        + _ffn_weight_specs(d, wg.shape[1]) + [_resident((1, d))],
        out_specs=[pl.BlockSpec((t, d), lambda i, *_: (jnp.minimum(i, n_out[0] - 1), 0)),
                   pl.BlockSpec((t, d), lambda i, *_: (jnp.maximum(i - n_out[0], 0), 0))],
        scratch_shapes=[pltpu.VMEM((2 * t, d), F32)],
    )
    return pl.pallas_call(
        functools.partial(_ffn_last_body, n_first=n_out[0]),
        grid_spec=grid_spec,
        out_shape=[jax.ShapeDtypeStruct((r, d), F32) for r in layout.real],
        compiler_params=_params(),
        name="ffn_last",
    )(blk, e, h, h, gpre, gpost, wg, wu, wd, gfinal)


def _conv_in_body(x_ref, gpre_ref, win_ref, gb_ref, z_ref):
    d = x_ref.shape[1]
    u = _rms(x_ref[...], gpre_ref[...]).astype(BF16)
    p = jnp.dot(u, win_ref[...], preferred_element_type=F32)
    gb_ref[...] = p[:, :d]
    z_ref[...] = p[:, d:2 * d] * p[:, 2 * d:]


def _conv_in(x, gpre, w_in):
    nt, d = x.shape
    tm = _row_tile(nt)
    row = pl.BlockSpec((tm, d), lambda i: (i, 0))
    return pl.pallas_call(
        _conv_in_body,
        grid=(nt // tm,),
        in_specs=[row, _resident((1, d)), _resident((d, 3 * d))],
        out_specs=[row, row],
        out_shape=[jax.ShapeDtypeStruct((nt, d), F32)] * 2,
        compiler_params=_params(),
        name="conv_in",
    )(x, gpre, w_in)


def _conv_out_body(so_ref, eo_ref, z_ref, zp_ref, zn_ref, gb_ref, h_ref, wc_ref, wout_ref,
                   gpost_ref, o_ref):
    i = pl.program_id(0)
    tm = z_ref.shape[0]
    z = z_ref[...]
    row = lax.broadcasted_iota(jnp.int32, (tm, 1), 0)
    zm1 = jnp.where(row == 0, zp_ref[7:8, :], pltpu.roll(z, 1, 0))
    zm1 = jnp.where(row == so_ref[i], 0.0, zm1)
    zp1 = jnp.where(row == tm - 1, zn_ref[0:1, :], pltpu.roll(z, tm - 1, 0))
    zp1 = jnp.where(row == eo_ref[i], 0.0, zp1)
    conv = wc_ref[0:1, :] * zm1 + wc_ref[1:2, :] * z + wc_ref[2:3, :] * zp1
    y = jnp.dot((gb_ref[...] * conv).astype(BF16), wout_ref[...], preferred_element_type=F32)
    o_ref[...] = h_ref[...] + _rms(y, gpost_ref[...])


def _conv_out(z, gb, h, w_conv, w_out, gpost, start_off, end_off):
    nt, d = z.shape
    tm = _row_tile(nt)
    sub = tm // 8
    n_sub = nt // 8
    row = pl.BlockSpec((tm, d), lambda i, *_: (i, 0))
    prev8 = pl.BlockSpec((8, d), lambda i, *_: (jnp.maximum(i * sub - 1, 0), 0))
    next8 = pl.BlockSpec((8, d), lambda i, *_: (jnp.minimum((i + 1) * sub, n_sub - 1), 0))
    grid_spec = pltpu.PrefetchScalarGridSpec(
        num_scalar_prefetch=2,
        grid=(nt // tm,),
        in_specs=[row, prev8, next8, row, row, _resident(w_conv.shape), _resident((d, d)),
                  _resident((1, d))],
        out_specs=row,
    )
    return pl.pallas_call(
        _conv_out_body,
        grid_spec=grid_spec,
        out_shape=jax.ShapeDtypeStruct((nt, d), F32),
        compiler_params=_params(),
        name="conv_out",
    )(start_off, end_off, z, z, z, gb, h, w_conv, w_out, gpost)


def _log_gates(z, lb):
    log_sig = jnp.minimum(z, 0.0) - jnp.log(1.0 + _exp_neg(jnp.abs(z)))
    a = jnp.log(lb)
    b = jnp.log1p(-lb) + log_sig
    logf = jnp.maximum(a, b) + jnp.log(1.0 + _exp_neg(jnp.abs(a - b)))
    key = (1.0 - lb) * _exp_neg(z - log_sig)
    return logf, key


def _split_hi_lo(x):
    hi = x.astype(BF16)
    lo = (x - hi.astype(F32)).astype(BF16)
    return hi, lo


def _hg_in_body(so_ref, eo_ref, x_ref, gpre_ref, win_ref, logit_ref,
                q_ref, v_ref, g_ref, kf_ref, kb_ref, hlf_ref, hlb_ref, *, layer):
    i = pl.program_id(0)
    tm, d = x_ref.shape
    u = _rms(x_ref[...], gpre_ref[...]).astype(BF16)
    depth = logit_ref.shape[0] // 2
    row = lax.broadcasted_iota(jnp.int32, (tm, 1), 0)

    def proj(j, cols):
        return jnp.dot(u, win_ref[:, j * d + cols.start:j * d + cols.stop],
                       preferred_element_type=F32)

    for cols in (slice(j, min(j + MXU_TILE, d)) for j in range(0, d, MXU_TILE)):
        q_ref[:, cols] = _silu(proj(0, cols)).astype(BF16)
        v_ref[:, cols] = proj(1, cols).astype(BF16)
        g_ref[:, cols] = _silu(proj(4, cols)).astype(BF16)
        for direction, (k_ref, hl_ref, off_ref) in enumerate(
                ((kf_ref, hlf_ref, so_ref), (kb_ref, hlb_ref, eo_ref))):
            logits = [logit_ref[2 * j + direction:2 * j + direction + 1, cols]
                      for j in range(depth)]
            m = functools.reduce(jnp.maximum, logits)
            e = [jnp.exp(l - m) for l in logits]
            lb = sum(e[1:layer + 1]) / sum(e)
            logf, key = _log_gates(proj(2 + direction, cols), lb)
            logf = jnp.where(row == off_ref[i], NEG_BIG, logf)
            k_ref[:, cols] = key.astype(BF16)
            hi, lo = _split_hi_lo(logf)
            for c in range(tm // CHUNK):
                hl_ref[2 * c * CHUNK:(2 * c + 1) * CHUNK, cols] = hi[c * CHUNK:(c + 1) * CHUNK]
                hl_ref[(2 * c + 1) * CHUNK:(2 * c + 2) * CHUNK, cols] = lo[c * CHUNK:(c + 1) * CHUNK]


def _hg_in(x, gpre, w_in, logits, start_off, end_off, layer):
    nt, d = x.shape
    tm = _row_tile(nt)
    row = pl.BlockSpec((tm, d), lambda i, *_: (i, 0))
    row2 = pl.BlockSpec((2 * tm, d), lambda i, *_: (i, 0))
    grid_spec = pltpu.PrefetchScalarGridSpec(
        num_scalar_prefetch=2,
        grid=(nt // tm,),
        in_specs=[row, _resident((1, d)), _resident((d, 5 * d)), _resident(logits.shape)],
        out_specs=[row] * 5 + [row2] * 2,
    )
    return pl.pallas_call(
        functools.partial(_hg_in_body, layer=layer),
        grid_spec=grid_spec,
        out_shape=[jax.ShapeDtypeStruct((nt, d), BF16)] * 5
        + [jax.ShapeDtypeStruct((2 * nt, d), BF16)] * 2,
        compiler_params=_params(),
        name="hg_in",
    )(start_off, end_off, x, gpre, w_in, logits)


def _scan_tables():
    c = CHUNK
    w = np.zeros((N_LEVELS + 2, c, c), np.float32)
    level = np.full((c, c), N_LEVELS + 1, np.int32)
    for l in range(N_LEVELS):
        m = c >> (l + 1)
        for r in range(c):
            n = (r // (2 * m)) * 2 * m + m - 1
            lo, hi = (r, n) if r <= n else (n, r)
            w[l, r, lo + 1:hi + 1] = 1.0
        for t in range(c):
            for s in range(c):
                if t // (2 * m) == s // (2 * m) and (t // m) % 2 == 1 and (s // m) % 2 == 0:
                    level[t, s] = l
    for r in range(c):
        w[N_LEVELS, r, :r + 1] = 1.0
        w[N_LEVELS + 1, r, r + 1:] = 1.0
        level[r, r] = N_LEVELS
    keep = [N_LEVELS] + list(range(N_VPU_LEVELS, N_LEVELS))
    w_f = np.tile(w[keep].reshape(-1, c), (1, 2))
    w_b = np.tile(w[keep][:, ::-1, ::-1].reshape(-1, c), (1, 2))
    return (jnp.asarray(w_f, BF16), jnp.asarray(w_b, BF16),
            jnp.asarray(level), jnp.asarray(level[::-1, ::-1].copy()))


def _reference_rows(level, backward):
    c = CHUNK
    if level == N_LEVELS + 1:
        return [(0, c, 0 if backward else c - 1)]
    m = c >> (level + 1)
    return [(g, g + 2 * m, g + (m if backward else m - 1)) for g in range(0, c, 2 * m)]


def _scan_body(wf_ref, wb_ref, lvf_ref, lvb_ref,
               qf_ref, vf_ref, kf_ref, hlf_ref,
               qb_ref, vb_ref, kb_ref, hlb_ref,
               of_ref, ob_ref, s_ref, e_ref, dec_ref, *, n_heads):
    tb, d = qf_ref.shape
    n_chunks = tb // CHUNK
    c = CHUNK

    @pl.when(pl.program_id(0) == 0)
    def _():
        s_ref[...] = jnp.zeros_like(s_ref)

    dirs = (
        (0, wf_ref, lvf_ref, qf_ref, vf_ref, kf_ref, hlf_ref, of_ref, c - 1),
        (1, wb_ref, lvb_ref, qb_ref, vb_ref, kb_ref, hlb_ref, ob_ref, 0),
    )
    masks = []
    for _, _, lv_ref, *_ in dirs:
        lv = lv_ref[...]
        masks.append([lv == l for l in range(N_LEVELS + 1)])
    contract_last = (((1,), (1,)), ((), ()))
    contract_rows = (((0,), (0,)), ((), ()))
    heads = [slice(h * HEAD_DIM, (h + 1) * HEAD_DIM) for h in range(n_heads)]
    lane_blocks = [slice(j, min(j + MXU_TILE, d)) for j in range(0, d, MXU_TILE)]

    def chunk_index(di, ci):
        return ci if di == 0 else n_chunks - 1 - ci

    def decay_factors(buf, ci):
        for di, w_ref, _, _, _, _, hl_ref, _, total_row in dirs:
            start = pl.multiple_of(chunk_index(di, ci) * 2 * c, 2 * c)
            for lanes in lane_blocks:
                x = jnp.dot(w_ref[...], hl_ref[pl.ds(start, 2 * c), lanes],
                            preferred_element_type=F32)
                b = x[0:c]
                e_b = jnp.exp(b)
                e_ref[buf, di, N_LEVELS * c:(N_LEVELS + 1) * c, lanes] = e_b.astype(BF16)
                dec_ref[buf, di, :, lanes] = e_b[total_row:total_row + 1, :]
                e_ref[buf, di, N_VPU_LEVELS * c:N_LEVELS * c, lanes] = jnp.exp(x[c:]).astype(BF16)
                for l in list(range(N_VPU_LEVELS)) + [N_LEVELS + 1]:
                    ref = jnp.concatenate(
                        [jnp.broadcast_to(b[n:n + 1, :], (hi - lo, b.shape[1]))
                         for lo, hi, n in _reference_rows(l, di == 1)], axis=0)
                    e_ref[buf, di, l * c:(l + 1) * c, lanes] = (
                        _exp_neg(jnp.abs(b - ref)).astype(BF16))

    def recurrence(buf, ci):
        def factor(di, group, sl):
            return e_ref[buf, di, group * c:(group + 1) * c, sl]

        inter, scores = {}, {}
        rows = [pl.ds(pl.multiple_of(chunk_index(di, ci) * c, c), c) for di in range(2)]
        for di, _, _, q_ref, v_ref, k_ref, _, _, _ in dirs:
            for h, sl in enumerate(heads):
                q, k = q_ref[rows[di], sl], k_ref[rows[di], sl]
                q_in = q * factor(di, N_LEVELS, sl)
                k_out = k * factor(di, N_LEVELS + 1, sl)
                st = s_ref[di * n_heads + h]
                inter[di, h] = lax.dot_general(q_in, st.astype(BF16), contract_last,
                                               preferred_element_type=F32)
                ds = lax.dot_general(v_ref[rows[di], sl], k_out, contract_rows,
                                     preferred_element_type=F32)
                s_ref[di * n_heads + h] = st * dec_ref[buf, di, :, sl] + ds
                sc = jnp.zeros((c, c), F32)
                for l in range(N_LEVELS + 1):
                    if l < N_LEVELS:
                        f = factor(di, l, sl)
                        a = lax.dot_general(q * f, k * f, contract_last,
                                            preferred_element_type=F32)
                    else:
                        a = lax.dot_general(q, k, contract_last, preferred_element_type=F32)
                    sc = jnp.where(masks[di][l], a, sc)
                scores[di, h] = sc.astype(BF16)
        for di, _, _, _, v_ref, _, _, o_ref, _ in dirs:
            for h, sl in enumerate(heads):
                intra = jnp.dot(scores[di, h], v_ref[rows[di], sl], preferred_element_type=F32)
                o_ref[rows[di], sl] = inter[di, h] + intra

    decay_factors(0, 0)

    def step_pair(j, carry):
        for half in range(2):
            ci = 2 * j + half
            decay_factors(1 - half, jnp.minimum(ci + 1, n_chunks - 1))
            recurrence(half, ci)
        return carry

    assert n_chunks % 2 == 0
    lax.fori_loop(0, n_chunks // 2, step_pair, 0)


def _scan(q, v, kf, hlf, kb, hlb):
    nt, d = q.shape
    tb = _row_tile(nt)
    nb = nt // tb
    n_heads = d // HEAD_DIM
    w_f, w_b, lv_f, lv_b = _scan_tables()
    fwd = pl.BlockSpec((tb, d), lambda i: (i, 0))
    bwd = pl.BlockSpec((tb, d), lambda i: (nb - 1 - i, 0))
    fwd2 = pl.BlockSpec((2 * tb, d), lambda i: (i, 0))
    bwd2 = pl.BlockSpec((2 * tb, d), lambda i: (nb - 1 - i, 0))
    return pl.pallas_call(
        functools.partial(_scan_body, n_heads=n_heads),
        grid=(nb,),
        in_specs=[_resident(w_f.shape), _resident(w_b.shape), _resident(lv_f.shape),
                  _resident(lv_b.shape), fwd, fwd, fwd, fwd2, bwd, bwd, bwd, bwd2],
        out_specs=[fwd, bwd],
        out_shape=[jax.ShapeDtypeStruct((nt, d), F32)] * 2,
        scratch_shapes=[
            pltpu.VMEM((2 * n_heads, HEAD_DIM, HEAD_DIM), F32),
            pltpu.VMEM((2, 2, (N_LEVELS + 2) * CHUNK, d), BF16),
            pltpu.VMEM((2, 2, 1, d), F32),
        ],
        compiler_params=_params(),
        name="hg_scan",
    )(w_f, w_b, lv_f, lv_b, q, v, kf, hlf, q, v, kb, hlb)


def _hg_out_body(of_ref, ob_ref, g_ref, h_ref, gn_ref, wout_ref, gpost_ref, o_ref):
    d = h_ref.shape[1]
    gated = (of_ref[...] + ob_ref[...]) * g_ref[...].astype(F32)
    parts = []
    for h in range(d // HEAD_DIM):
        gh = gated[:, h * HEAD_DIM:(h + 1) * HEAD_DIM]
        parts.append(gh * lax.rsqrt(jnp.mean(gh * gh, axis=-1, keepdims=True) + EPS))
    normed = jnp.concatenate(parts, axis=-1) * gn_ref[...]
    y = jnp.dot(normed.astype(BF16), wout_ref[...], preferred_element_type=F32)
    o_ref[...] = h_ref[...] + _rms(y, gpost_ref[...])


def _hg_out(o_f, o_b, g, h, gn, w_out, gpost):
    nt, d = h.shape
    tm = _row_tile(nt)
    row = pl.BlockSpec((tm, d), lambda i: (i, 0))
    vec = _resident((1, d))
    return pl.pallas_call(
        _hg_out_body,
        grid=(nt // tm,),
        in_specs=[row, row, row, row, vec, _resident((d, d)), vec],
        out_specs=row,
        out_shape=jax.ShapeDtypeStruct((nt, d), F32),
        compiler_params=_params(),
        name="hg_out",
    )(o_f, o_b, g, h, gn, w_out, gpost)


def _boundary_tables(seq_lens, n_rows, tile):
    n_tiles = n_rows // tile
    start = np.full((n_tiles,), -1, np.int32)
    end = np.full((n_tiles,), -1, np.int32)
    base = 0
    for length in seq_lens:
        assert length >= 2 * tile, "at most one sequence boundary per row tile"
        start[base // tile] = base % tile
        last = base + length - 1
        end[last // tile] = last % tile
        base += length
    return jnp.asarray(start), jnp.asarray(end)


def kernel(x_prompt, x_sample, meta_tokens, norm_pre, norm_post, ffn_w_gate, ffn_w_up, ffn_w_down,
           sc_w_in, sc_conv, sc_w_out, hg_w_in, hg_lb_logits, hg_gn, hg_w_out, final_norm):
    d = x_prompt.shape[-1]
    depth = norm_pre.shape[0]
    assert d % HEAD_DIM == 0

    layout = _Layout([x_prompt.shape, x_sample.shape])
    x_flat = [x.reshape(-1, d).astype(F32) for x in (x_prompt, x_sample)]
    start_off, end_off = _boundary_tables(layout.length, layout.n_rows, layout.tile)

    vec = lambda g: g.reshape(1, d).astype(F32)
    w16 = lambda w: w.astype(BF16)
    logits = hg_lb_logits.reshape(2 * depth, -1).astype(F32)

    h = None
    for i in range(depth):
        ffn1 = (vec(norm_pre[i, 0]), vec(norm_post[i, 0]),
                w16(ffn_w_gate[i, 0]), w16(ffn_w_up[i, 0]), w16(ffn_w_down[i, 0]))
        if i == 0:
            h = _ffn_first(layout, x_flat, meta_tokens.astype(F32), *ffn1)
        else:
            h = _ffn(h, *ffn1)
        j = i // 2
        if i % 2 == 0:
            gb, z = _conv_in(h, vec(norm_pre[i, 1]), w16(sc_w_in[j]))
            h = _conv_out(z, gb, h, sc_conv[j].astype(F32), w16(sc_w_out[j]), vec(norm_post[i, 1]),
                          start_off, end_off)
        else:
            q, v, g, kf, kb, hlf, hlb = _hg_in(
                h, vec(norm_pre[i, 1]), w16(hg_w_in[j]), logits, start_off, end_off, layer=i)
            o_f, o_b = _scan(q, v, kf, hlf, kb, hlb)
            h = _hg_out(o_f, o_b, g, h, vec(hg_gn[j]), w16(hg_w_out[j]), vec(norm_post[i, 1]))
        ffn2 = (vec(norm_pre[i, 2]), vec(norm_post[i, 2]),
                w16(ffn_w_gate[i, 1]), w16(ffn_w_up[i, 1]), w16(ffn_w_down[i, 1]))
        if i < depth - 1:
            h = _ffn(h, *ffn2)
    y_prompt, y_sample = _ffn_last(layout, h, *ffn2, vec(final_norm))
    return y_prompt.reshape(x_prompt.shape), y_sample.reshape(x_sample.shape)
```

```python
import functools

import numpy as np
import jax
import jax.numpy as jnp
from jax import lax
from jax.experimental import pallas as pl
from jax.experimental.pallas import tpu as pltpu

EPS = 1e-6
N_META = 16
HEAD_DIM = 128
CHUNK = 64
N_LEVELS = 6
N_VPU_LEVELS = 4
NEG_BIG = -1000.0
LOG2E = 1.4426950408889634
MXU_TILE = 256
SUBLANES = 8
VMEM_LIMIT_BYTES = 56 * 1024 * 1024

F32 = jnp.float32
BF16 = jnp.bfloat16


def _rms(x, g):
    return x * lax.rsqrt(jnp.mean(x * x, axis=-1, keepdims=True) + EPS) * g


def _exp_neg(x):
    return jnp.exp2(x * (-LOG2E))


def _silu(x):
    return x / (1.0 + _exp_neg(x))


def _pack(x):
    return pltpu.bitcast(x, jnp.uint32)


def _unpack(x):
    return pltpu.bitcast(x, BF16)


def _packed_rows(rows):
    return rows * jnp.dtype(BF16).itemsize // 4


def _row_tile(n_rows):
    return 512 if n_rows >= 4096 else 128


def _resident(shape):
    nd = len(shape)
    return pl.BlockSpec(shape, lambda i, *_: (0,) * nd, pipeline_mode=pl.Buffered(1))


def _params():
    return pltpu.CompilerParams(dimension_semantics=("arbitrary",),
                                vmem_limit_bytes=VMEM_LIMIT_BYTES)


def _ffn_splits(d_ff):
    if d_ff % MXU_TILE or d_ff < 4 * MXU_TILE:
        return ((0, d_ff),)
    half = (d_ff // MXU_TILE + 1) // 2 * MXU_TILE
    return ((0, half), (half, d_ff))


def _ffn_math(x, gpre_ref, gpost_ref, wg_ref, wu_ref, wd_ref):
    u = _rms(x, gpre_ref[...]).astype(BF16)
    f = None
    for lo, hi in _ffn_splits(wg_ref.shape[1]):
        a = jnp.dot(u, wg_ref[:, lo:hi], preferred_element_type=F32)
        b = jnp.dot(u, wu_ref[:, lo:hi], preferred_element_type=F32)
        hdn = (_silu(a) * b).astype(BF16)
        part = jnp.dot(hdn, wd_ref[lo:hi, :], preferred_element_type=F32)
        f = part if f is None else f + part
    return x + 0.5 * _rms(f, gpost_ref[...])


def _ffn_weight_specs(d, d_ff):
    vec = _resident((1, d))
    return [vec, vec, _resident((d, d_ff)), _resident((d, d_ff)), _resident((d_ff, d))]


def _ffn_body(x_ref, gpre_ref, gpost_ref, wg_ref, wu_ref, wd_ref, o_ref):
    o_ref[...] = _ffn_math(x_ref[...], gpre_ref, gpost_ref, wg_ref, wu_ref, wd_ref)


def _ffn(x, gpre, gpost, wg, wu, wd):
    nt, d = x.shape
    tm = _row_tile(nt)
    row = pl.BlockSpec((tm, d), lambda i: (i, 0))
    return pl.pallas_call(
        _ffn_body,
        grid=(nt // tm,),
        in_specs=[row] + _ffn_weight_specs(d, wg.shape[1]),
        out_specs=row,
        out_shape=jax.ShapeDtypeStruct((nt, d), F32),
        compiler_params=_params(),
        name="ffn",
    )(x, gpre, gpost, wg, wu, wd)


class _Layout:
    def __init__(self, shapes):
        self.arr, self.base, self.first, self.length = [], [], [], []
        self.real = [bn * s for bn, s, _ in shapes]
        row = 0
        for a, (bn, s, _) in enumerate(shapes):
            for b in range(bn):
                self.arr.append(a)
                self.base.append(row)
                self.first.append(b * s)
                self.length.append(N_META + s)
                row += N_META + s
        self.n_real = row
        self.tile = _row_tile(row)
        self.n_rows = -(-row // self.tile) * self.tile
        t = self.tile
        for a, (bn, s, _) in enumerate(shapes):
            assert s % t == 0 and s % 16 == 0 and N_META + s >= 2 * t, (s, t)

    def seq_of(self, row):
        return max(s for s in range(len(self.base)) if self.base[s] <= row)

    def gather_tables(self):
        t = self.tile
        n_tiles = self.n_rows // t
        cols = {k: np.zeros((n_tiles,), np.int32)
                for k in ("blk0", "blk1", "src1", "e1", "src2", "e2", "off", "valid")}
        held = [0, 0]
        for i in range(n_tiles):
            u0 = i * t
            starts = [s for s in range(len(self.base)) if u0 <= self.base[s] < u0 + t]
            assert len(starts) <= 1
            off = self.base[starts[0]] - u0 if starts else t
            windows = []
            if off > 0 and u0 < self.n_real:
                windows.append((1, self.seq_of(u0)))
            if starts:
                windows.append((2, starts[0]))
            pairs = {}
            for which, s in windows:
                w = u0 - self.base[s] - N_META + self.first[s]
                blk, e = w // t, w % t
                assert pairs.setdefault(self.arr[s], blk) == blk and e % 16 == 0
                cols["src%d" % which][i], cols["e%d" % which][i] = self.arr[s], e
            for a in range(2):
                held[a] = pairs.get(a, held[a])
                cols["blk%d" % a][i] = held[a]
            cols["off"][i] = off
            cols["valid"][i] = min(max(self.n_real - u0, 0), t)
        return [jnp.asarray(cols[k]) for k in
                ("blk0", "blk1", "src1", "e1", "src2", "e2", "off", "valid")]

    def scatter_tables(self):
        t = self.tile
        blk, e = [], []
        for s in range(len(self.base)):
            for j in range((self.length[s] - N_META) // t):
                r = self.base[s] + N_META + j * t
                blk.append(r // t)
                e.append(r % t)
                assert r % 16 == 0
        return jnp.asarray(blk, jnp.int32), jnp.asarray(e, jnp.int32)


def _ffn_first_body(b0_ref, b1_ref, s1_ref, e1_ref, s2_ref, e2_ref, off_ref, valid_ref,
                    pa_ref, pb_ref, sa_ref, sb_ref, meta_ref,
                    gpre_ref, gpost_ref, wg_ref, wu_ref, wd_ref, o_ref, pair_ref, x_ref):
    i = pl.program_id(0)
    t = o_ref.shape[0]
    pair_ref[0, 0:t, :] = pa_ref[...]
    pair_ref[0, t:2 * t, :] = pb_ref[...]
    pair_ref[1, 0:t, :] = sa_ref[...]
    pair_ref[1, t:2 * t, :] = sb_ref[...]

    def window(src, e):
        return pair_ref[src, pl.ds(pl.multiple_of(e, 16), t), :]

    row = lax.broadcasted_iota(jnp.int32, (t, 1), 0)
    x = jnp.where(row < off_ref[i], window(s1_ref[i], e1_ref[i]), window(s2_ref[i], e2_ref[i]))
    x_ref[...] = jnp.where(row < valid_ref[i], x, 0.0)

    @pl.when(off_ref[i] < t)
    def _():
        x_ref[pl.ds(pl.multiple_of(off_ref[i], 16), N_META), :] = meta_ref[...]

    o_ref[...] = _ffn_math(x_ref[...], gpre_ref, gpost_ref, wg_ref, wu_ref, wd_ref)


def _ffn_first(layout, x_flat, meta, gpre, gpost, wg, wu, wd):
    d = meta.shape[1]
    t = layout.tile
    tables = layout.gather_tables()
    n_blk = [x.shape[0] // t for x in x_flat]

    def block(a, second):
        def index(i, b0, b1, *_):
            return (jnp.clip((b0, b1)[a][i] + second, 0, n_blk[a] - 1), 0)
        return pl.BlockSpec((t, d), index)

    grid_spec = pltpu.PrefetchScalarGridSpec(
        num_scalar_prefetch=len(tables),
        grid=(layout.n_rows // t,),
        in_specs=[block(0, 0), block(0, 1), block(1, 0), block(1, 1), _resident((N_META, d))]
        + _ffn_weight_specs(d, wg.shape[1]),
        out_specs=pl.BlockSpec((t, d), lambda i, *_: (i, 0)),
        scratch_shapes=[pltpu.VMEM((2, 2 * t, d), F32), pltpu.VMEM((t, d), F32)],
    )
    return pl.pallas_call(
        _ffn_first_body,
        grid_spec=grid_spec,
        out_shape=jax.ShapeDtypeStruct((layout.n_rows, d), F32),
        compiler_params=_params(),
        name="ffn_first",
    )(*tables, x_flat[0], x_flat[0], x_flat[1], x_flat[1], meta, gpre, gpost, wg, wu, wd)


def _ffn_last_body(blk_ref, e_ref, ha_ref, hb_ref, gpre_ref, gpost_ref, wg_ref, wu_ref, wd_ref,
                   gfin_ref, yp_ref, ys_ref, pair_ref, *, n_first):
    i = pl.program_id(0)
    t = yp_ref.shape[0]
    pair_ref[0:t, :] = ha_ref[...]
    pair_ref[t:2 * t, :] = hb_ref[...]
    x = pair_ref[pl.ds(pl.multiple_of(e_ref[i], 16), t), :]
    y = _rms(_ffn_math(x, gpre_ref, gpost_ref, wg_ref, wu_ref, wd_ref), gfin_ref[...])

    @pl.when(i < n_first)
    def _():
        yp_ref[...] = y

    @pl.when(i >= n_first)
    def _():
        ys_ref[...] = y


def _ffn_last(layout, h, gpre, gpost, wg, wu, wd, gfinal):
    d = h.shape[1]
    t = layout.tile
    blk, e = layout.scatter_tables()
    n_out = [r // t for r in layout.real]
    n_tiles = layout.n_rows // t
    grid_spec = pltpu.PrefetchScalarGridSpec(
        num_scalar_prefetch=2,
        grid=(n_out[0] + n_out[1],),
        in_specs=[pl.BlockSpec((t, d), lambda i, blk, e: (blk[i], 0)),
                  pl.BlockSpec((t, d), lambda i, blk, e: (jnp.minimum(blk[i] + 1, n_tiles - 1), 0))]
        + _ffn_weight_specs(d, wg.shape[1]) + [_resident((1, d))],
        out_specs=[pl.BlockSpec((t, d), lambda i, *_: (jnp.minimum(i, n_out[0] - 1), 0)),
                   pl.BlockSpec((t, d), lambda i, *_: (jnp.maximum(i - n_out[0], 0), 0))],
        scratch_shapes=[pltpu.VMEM((2 * t, d), F32)],
    )
    return pl.pallas_call(
        functools.partial(_ffn_last_body, n_first=n_out[0]),
        grid_spec=grid_spec,
        out_shape=[jax.ShapeDtypeStruct((r, d), F32) for r in layout.real],
        compiler_params=_params(),
        name="ffn_last",
    )(blk, e, h, h, gpre, gpost, wg, wu, wd, gfinal)


def _conv_in_body(x_ref, gpre_ref, win_ref, gb_ref, z_ref):
    d = x_ref.shape[1]
    u = _rms(x_ref[...], gpre_ref[...]).astype(BF16)
    p = jnp.dot(u, win_ref[...], preferred_element_type=F32)
    gb_ref[...] = p[:, :d]
    z_ref[...] = p[:, d:2 * d] * p[:, 2 * d:]


def _conv_in(x, gpre, w_in):
    nt, d = x.shape
    tm = _row_tile(nt)
    row = pl.BlockSpec((tm, d), lambda i: (i, 0))
    return pl.pallas_call(
        _conv_in_body,
        grid=(nt // tm,),
        in_specs=[row, _resident((1, d)), _resident((d, 3 * d))],
        out_specs=[row, row],
        out_shape=[jax.ShapeDtypeStruct((nt, d), F32)] * 2,
        compiler_params=_params(),
        name="conv_in",
    )(x, gpre, w_in)


def _conv_out_body(so_ref, eo_ref, z_ref, zp_ref, zn_ref, gb_ref, h_ref, wc_ref, wout_ref,
                   gpost_ref, o_ref):
    i = pl.program_id(0)
    tm = z_ref.shape[0]
    z = z_ref[...]
    row = lax.broadcasted_iota(jnp.int32, (tm, 1), 0)
    zm1 = jnp.where(row == 0, zp_ref[7:8, :], pltpu.roll(z, 1, 0))
    zm1 = jnp.where(row == so_ref[i], 0.0, zm1)
    zp1 = jnp.where(row == tm - 1, zn_ref[0:1, :], pltpu.roll(z, tm - 1, 0))
    zp1 = jnp.where(row == eo_ref[i], 0.0, zp1)
    conv = wc_ref[0:1, :] * zm1 + wc_ref[1:2, :] * z + wc_ref[2:3, :] * zp1
    y = jnp.dot((gb_ref[...] * conv).astype(BF16), wout_ref[...], preferred_element_type=F32)
    o_ref[...] = h_ref[...] + _rms(y, gpost_ref[...])


def _conv_out(z, gb, h, w_conv, w_out, gpost, start_off, end_off):
    nt, d = z.shape
    tm = _row_tile(nt)
    sub = tm // 8
    n_sub = nt // 8
    row = pl.BlockSpec((tm, d), lambda i, *_: (i, 0))
    prev8 = pl.BlockSpec((8, d), lambda i, *_: (jnp.maximum(i * sub - 1, 0), 0))
    next8 = pl.BlockSpec((8, d), lambda i, *_: (jnp.minimum((i + 1) * sub, n_sub - 1), 0))
    grid_spec = pltpu.PrefetchScalarGridSpec(
        num_scalar_prefetch=2,
        grid=(nt // tm,),
        in_specs=[row, prev8, next8, row, row, _resident(w_conv.shape), _resident((d, d)),
                  _resident((1, d))],
        out_specs=row,
    )
    return pl.pallas_call(
        _conv_out_body,
        grid_spec=grid_spec,
        out_shape=jax.ShapeDtypeStruct((nt, d), F32),
        compiler_params=_params(),
        name="conv_out",
    )(start_off, end_off, z, z, z, gb, h, w_conv, w_out, gpost)


def _log_gates(z, lb):
    t = _exp_neg(jnp.abs(z))
    pos = z >= 0.0
    r = 1.0 / (1.0 + t)
    f = jnp.where(pos, 1.0 + lb * t, lb + t) * r
    logf = jnp.maximum(jnp.log(f), NEG_BIG)
    key = (1.0 - lb) * (jnp.where(pos, t, 1.0) * r)
    return logf, key


def _split_hi_lo(x):
    hi = x.astype(BF16)
    lo = (x - hi.astype(F32)).astype(BF16)
    return hi, lo


def _hg_in_body(so_ref, eo_ref, x_ref, gpre_ref, win_ref, logit_ref,
                q_ref, v_ref, g_ref, kf_ref, kb_ref, hlf_ref, hlb_ref, *, layer):
    i = pl.program_id(0)
    tm, d = x_ref.shape
    u = _rms(x_ref[...], gpre_ref[...]).astype(BF16)
    depth = logit_ref.shape[0] // 2
    row = lax.broadcasted_iota(jnp.int32, (tm, 1), 0)

    def proj(j, cols):
        return jnp.dot(u, win_ref[:, j * d + cols.start:j * d + cols.stop],
                       preferred_element_type=F32)

    for cols in (slice(j, min(j + MXU_TILE, d)) for j in range(0, d, MXU_TILE)):
        q_ref[:, cols] = _pack(_silu(proj(0, cols)).astype(BF16))
        v_ref[:, cols] = _pack(proj(1, cols).astype(BF16))
        g_ref[:, cols] = _silu(proj(4, cols)).astype(BF16)
        for direction, (k_ref, hl_ref, off_ref) in enumerate(
                ((kf_ref, hlf_ref, so_ref), (kb_ref, hlb_ref, eo_ref))):
            logits = [logit_ref[2 * j + direction:2 * j + direction + 1, cols]
                      for j in range(depth)]
            m = functools.reduce(jnp.maximum, logits)
            e = [jnp.exp(l - m) for l in logits]
            lb = sum(e[1:layer + 1]) / sum(e)
            logf, key = _log_gates(proj(2 + direction, cols), lb)
            logf = jnp.where(row == off_ref[i], NEG_BIG, logf)
            k_ref[:, cols] = _pack(key.astype(BF16))
            hi, lo = _split_hi_lo(logf)
            for c in range(tm // CHUNK):
                hl_ref[2 * c * CHUNK:(2 * c + 1) * CHUNK, cols] = hi[c * CHUNK:(c + 1) * CHUNK]
                hl_ref[(2 * c + 1) * CHUNK:(2 * c + 2) * CHUNK, cols] = lo[c * CHUNK:(c + 1) * CHUNK]


def _hg_in(x, gpre, w_in, logits, start_off, end_off, layer):
    nt, d = x.shape
    tm = _row_tile(nt)
    row = pl.BlockSpec((tm, d), lambda i, *_: (i, 0))
    row2 = pl.BlockSpec((2 * tm, d), lambda i, *_: (i, 0))
    words = pl.BlockSpec((_packed_rows(tm), d), lambda i, *_: (i, 0))
    plain = jax.ShapeDtypeStruct((nt, d), BF16)
    packed = jax.ShapeDtypeStruct((_packed_rows(nt), d), jnp.uint32)
    grid_spec = pltpu.PrefetchScalarGridSpec(
        num_scalar_prefetch=2,
        grid=(nt // tm,),
        in_specs=[row, _resident((1, d)), _resident((d, 5 * d)), _resident(logits.shape)],
        out_specs=[words, words, row, words, words, row2, row2],
    )
    return pl.pallas_call(
        functools.partial(_hg_in_body, layer=layer),
        grid_spec=grid_spec,
        out_shape=[packed, packed, plain, packed, packed]
        + [jax.ShapeDtypeStruct((2 * nt, d), BF16)] * 2,
        compiler_params=_params(),
        name="hg_in",
    )(start_off, end_off, x, gpre, w_in, logits)


def _scan_tables():
    c = CHUNK
    w = np.zeros((N_LEVELS + 1, c, c), np.float32)
    for l in range(N_LEVELS):
        m = c >> (l + 1)
        for r in range(c):
            n = (r // (2 * m)) * 2 * m + m - 1
            lo, hi = (r, n) if r <= n else (n, r)
            w[l, r, lo + 1:hi + 1] = 1.0
    for r in range(c):
        w[N_LEVELS, r, :r + 1] = 1.0
    keep = [N_LEVELS] + list(range(N_VPU_LEVELS, N_LEVELS))
    w_f = np.tile(w[keep].reshape(-1, c), (1, 2))
    w_b = np.tile(w[keep][:, ::-1, ::-1].reshape(-1, c), (1, 2))
    lv_f, lv_b = _level_tables()
    return jnp.asarray(w_f, BF16), jnp.asarray(w_b, BF16), jnp.asarray(lv_f), jnp.asarray(lv_b)


def _level_tables():
    c = CHUNK
    level = np.full((c, c), N_LEVELS + 1, np.int32)
    for l in range(N_LEVELS):
        m = c >> (l + 1)
        for t in range(c):
            for s in range(c):
                if t // (2 * m) == s // (2 * m) and (t // m) % 2 == 1 and (s // m) % 2 == 0:
                    level[t, s] = l
    level[np.arange(c), np.arange(c)] = N_LEVELS
    return level, level[::-1, ::-1].copy()


def _reference_rows(level, backward):
    c = CHUNK
    if level == N_LEVELS + 1:
        return [(0, c, 0 if backward else c - 1)]
    m = c >> (level + 1)
    return [(g, g + 2 * m, g + (m if backward else m - 1)) for g in range(0, c, 2 * m)]


def _scan_body(wf_ref, wb_ref, lvf_ref, lvb_ref,
               qf_ref, vf_ref, kf_ref, hlf_ref,
               qb_ref, vb_ref, kb_ref, hlb_ref,
               of_ref, ob_ref, s_ref, e_ref, dec_ref, *, n_heads):
    tb, d = of_ref.shape
    n_chunks = tb // CHUNK
    c = CHUNK

    @pl.when(pl.program_id(0) == 0)
    def _():
        s_ref[...] = jnp.zeros_like(s_ref)

    dirs = (
        (0, wf_ref, lvf_ref, qf_ref, vf_ref, kf_ref, hlf_ref, of_ref, c - 1),
        (1, wb_ref, lvb_ref, qb_ref, vb_ref, kb_ref, hlb_ref, ob_ref, 0),
    )
    masks = []
    for _, _, lv_ref, *_ in dirs:
        lv = lv_ref[...]
        masks.append([[lv[p:p + SUBLANES] == l for p in range(0, c, SUBLANES)]
                      for l in range(N_LEVELS + 1)])
    owned = [[[bool((lv_np[p:p + SUBLANES] == l).any()) for p in range(0, c, SUBLANES)]
              for l in range(N_LEVELS + 1)] for lv_np in _level_tables()]
    parity = lax.broadcasted_iota(jnp.int32, (c, 1), 0) & 1
    query_side = [parity == 1, parity == 0]
    shifts = [1, c - 1]
    contract_last = (((1,), (1,)), ((), ()))
    contract_rows = (((0,), (0,)), ((), ()))
    heads = [slice(h * HEAD_DIM, (h + 1) * HEAD_DIM) for h in range(n_heads)]
    lane_blocks = [slice(j, min(j + MXU_TILE, d)) for j in range(0, d, MXU_TILE)]

    def chunk_index(di, ci):
        return ci if di == 0 else n_chunks - 1 - ci

    def store_factor(buf, di, group, lanes, value):
        for h in range(lanes.start // HEAD_DIM, lanes.stop // HEAD_DIM):
            lo = h * HEAD_DIM - lanes.start
            e_ref[buf, di, h, _packed_rows(group * c):_packed_rows((group + 1) * c), :] = (
                _pack(value[:, lo:lo + HEAD_DIM].astype(BF16)))

    def coarse_exponent(b, level, backward):
        tiles = []
        for lo, hi, n in _reference_rows(level, backward):
            ref = b[n:n + 1, :]
            for p in range(lo, hi, SUBLANES):
                rows = b[p:p + SUBLANES]
                before = p + SUBLANES - 1 < n if backward else p + SUBLANES - 1 <= n
                after = p >= n if backward else p > n
                if before:
                    tiles.append(rows - ref if backward else ref - rows)
                elif after:
                    tiles.append(ref - rows if backward else rows - ref)
                else:
                    tiles.append(-jnp.abs(rows - ref))
        return jnp.concatenate(tiles, axis=0)

    def decay_factors(buf, ci):
        for di, w_ref, _, _, _, _, hl_ref, _, total_row in dirs:
            start = pl.multiple_of(chunk_index(di, ci) * 2 * c, 2 * c)
            for lanes in lane_blocks:
                x = jnp.dot(w_ref[...], hl_ref[pl.ds(start, 2 * c), lanes],
                            preferred_element_type=F32)
                b = x[0:c]
                e_b = jnp.exp(b)
                store_factor(buf, di, N_LEVELS, lanes, e_b)
                dec_ref[buf, di, :, lanes] = e_b[total_row:total_row + 1, :]
                for j, l in enumerate(range(N_VPU_LEVELS, N_LEVELS)):
                    store_factor(buf, di, l, lanes, jnp.exp(x[(j + 1) * c:(j + 2) * c]))
                for l in list(range(N_VPU_LEVELS)) + [N_LEVELS + 1]:
                    store_factor(buf, di, l, lanes, jnp.exp(coarse_exponent(b, l, di == 1)))

    def recurrence(buf, ci):
        def factor(di, group, h):
            return _unpack(
                e_ref[buf, di, h, _packed_rows(group * c):_packed_rows((group + 1) * c), :])

        inter, scores, own, near = {}, {}, {}, {}
        rows = [pl.ds(pl.multiple_of(chunk_index(di, ci) * c, c), c) for di in range(2)]
        pc = _packed_rows(c)
        words = [pl.ds(pl.multiple_of(chunk_index(di, ci) * pc, pc), pc) for di in range(2)]
        for di, _, _, q_ref, v_ref, k_ref, _, _, _ in dirs:
            for h, sl in enumerate(heads):
                q, k = _unpack(q_ref[words[di], sl]), _unpack(k_ref[words[di], sl])
                q_in = q * factor(di, N_LEVELS, h)
                k_out = k * factor(di, N_LEVELS + 1, h)
                st = s_ref[di * n_heads + h]
                inter[di, h] = lax.dot_general(q_in, st.astype(BF16), contract_last,
                                               preferred_element_type=F32)
                ds = lax.dot_general(_unpack(v_ref[words[di], sl]), k_out, contract_rows,
                                     preferred_element_type=F32)
                s_ref[di * n_heads + h] = st * dec_ref[buf, di, :, sl] + ds
                own[di, h] = jnp.sum(q.astype(F32) * k.astype(F32), axis=-1, keepdims=True)
                f = factor(di, N_LEVELS - 1, h)
                pair = jnp.sum((q * f).astype(F32) * pltpu.roll((k * f).astype(F32), shifts[di], 0),
                               axis=-1, keepdims=True)
                near[di, h] = jnp.where(query_side[di], pair, 0.0)
                sc = [jnp.zeros((SUBLANES, c), F32)] * (c // SUBLANES)
                for l in range(N_LEVELS - 1):
                    f = factor(di, l, h)
                    a = lax.dot_general(q * f, k * f, contract_last, preferred_element_type=F32)
                    for p in range(c // SUBLANES):
                        if owned[di][l][p]:
                            sc[p] = jnp.where(masks[di][l][p],
                                              a[p * SUBLANES:(p + 1) * SUBLANES], sc[p])
                scores[di, h] = jnp.concatenate(sc, axis=0).astype(BF16)
        for di, _, _, _, v_ref, _, _, o_ref, _ in dirs:
            for h, sl in enumerate(heads):
                v = _unpack(v_ref[words[di], sl])
                intra = jnp.dot(scores[di, h], v, preferred_element_type=F32)
                v32 = v.astype(F32)
                o_ref[rows[di], sl] = (inter[di, h] + intra + own[di, h] * v32
                                       + near[di, h] * pltpu.roll(v32, shifts[di], 0))

    decay_factors(0, 0)

    def step_pair(j, carry):
        for half in range(2):
            ci = 2 * j + half
            decay_factors(1 - half, jnp.minimum(ci + 1, n_chunks - 1))
            recurrence(half, ci)
        return carry

    assert n_chunks % 2 == 0
    lax.fori_loop(0, n_chunks // 2, step_pair, 0)


def _scan(q, v, kf, hlf, kb, hlb):
    nt, d = hlf.shape[0] // 2, hlf.shape[1]
    tb = _row_tile(nt)
    nb = nt // tb
    n_heads = d // HEAD_DIM
    w_f, w_b, lv_f, lv_b = _scan_tables()
    fwd = pl.BlockSpec((tb, d), lambda i: (i, 0))
    bwd = pl.BlockSpec((tb, d), lambda i: (nb - 1 - i, 0))
    fwd2 = pl.BlockSpec((2 * tb, d), lambda i: (i, 0))
    bwd2 = pl.BlockSpec((2 * tb, d), lambda i: (nb - 1 - i, 0))
    fwdw = pl.BlockSpec((_packed_rows(tb), d), lambda i: (i, 0))
    bwdw = pl.BlockSpec((_packed_rows(tb), d), lambda i: (nb - 1 - i, 0))
    return pl.pallas_call(
        functools.partial(_scan_body, n_heads=n_heads),
        grid=(nb,),
        in_specs=[_resident(w_f.shape), _resident(w_b.shape), _resident(lv_f.shape),
                  _resident(lv_b.shape), fwdw, fwdw, fwdw, fwd2, bwdw, bwdw, bwdw, bwd2],
        out_specs=[fwd, bwd],
        out_shape=[jax.ShapeDtypeStruct((nt, d), F32)] * 2,
        scratch_shapes=[
            pltpu.VMEM((2 * n_heads, HEAD_DIM, HEAD_DIM), F32),
            pltpu.VMEM((2, 2, n_heads, _packed_rows((N_LEVELS + 2) * CHUNK), HEAD_DIM), jnp.uint32),
            pltpu.VMEM((2, 2, 1, d), F32),
        ],
        compiler_params=_params(),
        name="hg_scan",
    )(w_f, w_b, lv_f, lv_b, q, v, kf, hlf, q, v, kb, hlb)


def _hg_out_body(of_ref, ob_ref, g_ref, h_ref, gn_ref, wout_ref, gpost_ref, o_ref):
    d = h_ref.shape[1]
    gated = (of_ref[...] + ob_ref[...]) * g_ref[...].astype(F32)
    parts = []
    for h in range(d // HEAD_DIM):
        gh = gated[:, h * HEAD_DIM:(h + 1) * HEAD_DIM]
        parts.append(gh * lax.rsqrt(jnp.mean(gh * gh, axis=-1, keepdims=True) + EPS))
    normed = jnp.concatenate(parts, axis=-1) * gn_ref[...]
    y = jnp.dot(normed.astype(BF16), wout_ref[...], preferred_element_type=F32)
    o_ref[...] = h_ref[...] + _rms(y, gpost_ref[...])


def _hg_out(o_f, o_b, g, h, gn, w_out, gpost):
    nt, d = h.shape
    tm = _row_tile(nt)
    row = pl.BlockSpec((tm, d), lambda i: (i, 0))
    vec = _resident((1, d))
    return pl.pallas_call(
        _hg_out_body,
        grid=(nt // tm,),
        in_specs=[row, row, row, row, vec, _resident((d, d)), vec],
        out_specs=row,
        out_shape=jax.ShapeDtypeStruct((nt, d), F32),
        compiler_params=_params(),
        name="hg_out",
    )(o_f, o_b, g, h, gn, w_out, gpost)


def _boundary_tables(seq_lens, n_rows, tile):
    n_tiles = n_rows // tile
    start = np.full((n_tiles,), -1, np.int32)
    end = np.full((n_tiles,), -1, np.int32)
    base = 0
    for length in seq_lens:
        assert length >= 2 * tile, "at most one sequence boundary per row tile"
        start[base // tile] = base % tile
        last = base + length - 1
        end[last // tile] = last % tile
        base += length
    return jnp.asarray(start), jnp.asarray(end)


def kernel(x_prompt, x_sample, meta_tokens, norm_pre, norm_post, ffn_w_gate, ffn_w_up, ffn_w_down,
           sc_w_in, sc_conv, sc_w_out, hg_w_in, hg_lb_logits, hg_gn, hg_w_out, final_norm):
    d = x_prompt.shape[-1]
    depth = norm_pre.shape[0]
    assert d % HEAD_DIM == 0

    layout = _Layout([x_prompt.shape, x_sample.shape])
    x_flat = [x.reshape(-1, d).astype(F32) for x in (x_prompt, x_sample)]
    start_off, end_off = _boundary_tables(layout.length, layout.n_rows, layout.tile)

    vec = lambda g: g.reshape(1, d).astype(F32)
    w16 = lambda w: w.astype(BF16)
    logits = hg_lb_logits.reshape(2 * depth, -1).astype(F32)

    h = None
    for i in range(depth):
        ffn1 = (vec(norm_pre[i, 0]), vec(norm_post[i, 0]),
                w16(ffn_w_gate[i, 0]), w16(ffn_w_up[i, 0]), w16(ffn_w_down[i, 0]))
        if i == 0:
            h = _ffn_first(layout, x_flat, meta_tokens.astype(F32), *ffn1)
        else:
            h = _ffn(h, *ffn1)
        j = i // 2
        if i % 2 == 0:
            gb, z = _conv_in(h, vec(norm_pre[i, 1]), w16(sc_w_in[j]))
            h = _conv_out(z, gb, h, sc_conv[j].astype(F32), w16(sc_w_out[j]), vec(norm_post[i, 1]),
                          start_off, end_off)
        else:
            q, v, g, kf, kb, hlf, hlb = _hg_in(
                h, vec(norm_pre[i, 1]), w16(hg_w_in[j]), logits, start_off, end_off, layer=i)
            o_f, o_b = _scan(q, v, kf, hlf, kb, hlb)
            h = _hg_out(o_f, o_b, g, h, vec(hg_gn[j]), w16(hg_w_out[j]), vec(norm_post[i, 1]))
        ffn2 = (vec(norm_pre[i, 2]), vec(norm_post[i, 2]),
                w16(ffn_w_gate[i, 1]), w16(ffn_w_up[i, 1]), w16(ffn_w_down[i, 1]))
        if i < depth - 1:
            h = _ffn(h, *ffn2)
    y_prompt, y_sample = _ffn_last(layout, h, *ffn2, vec(final_norm))
    return y_prompt.reshape(x_prompt.shape), y_sample.reshape(x_sample.shape)
```

```python
import functools

import numpy as np
import jax
import jax.numpy as jnp
from jax import lax
from jax.experimental import pallas as pl
from jax.experimental.pallas import tpu as pltpu

EPS = 1e-6
N_META = 16
HEAD_DIM = 128
CHUNK = 64
N_LEVELS = 6
N_VPU_LEVELS = 4
NEG_BIG = -1000.0
LOG2E = 1.4426950408889634
MXU_TILE = 256
SUBLANES = 8
VMEM_LIMIT_BYTES = 56 * 1024 * 1024

F32 = jnp.float32
BF16 = jnp.bfloat16


def _rms(x, g):
    return x * lax.rsqrt(jnp.mean(x * x, axis=-1, keepdims=True) + EPS) * g


def _exp_neg(x):
    return jnp.exp2(x * (-LOG2E))


def _silu(x):
    return x / (1.0 + _exp_neg(x))


def _pack(x):
    return pltpu.bitcast(x, jnp.uint32)


def _unpack(x):
    return pltpu.bitcast(x, BF16)


def _packed_rows(rows):
    return rows * jnp.dtype(BF16).itemsize // 4


def _row_tile(n_rows):
    return 512 if n_rows >= 4096 else 128


def _resident(shape):
    nd = len(shape)
    return pl.BlockSpec(shape, lambda i, *_: (0,) * nd, pipeline_mode=pl.Buffered(1))


def _params(**flags):
    return pltpu.CompilerParams(dimension_semantics=("arbitrary",),
                                vmem_limit_bytes=VMEM_LIMIT_BYTES, flags=flags or None)


def _ffn_splits(d_ff):
    if d_ff % MXU_TILE or d_ff < 4 * MXU_TILE:
        return ((0, d_ff),)
    step = 1 * MXU_TILE
    return tuple((lo, min(lo + step, d_ff)) for lo in range(0, d_ff, step))


def _ffn_math(x, gpre_ref, gpost_ref, wg_ref, wu_ref, wd_ref):
    u = _rms(x, gpre_ref[...]).astype(BF16)
    f = None
    for lo, hi in _ffn_splits(wg_ref.shape[1]):
        a = jnp.dot(u, wg_ref[:, lo:hi], preferred_element_type=F32)
        b = jnp.dot(u, wu_ref[:, lo:hi], preferred_element_type=F32)
        hdn = (_silu(a) * b).astype(BF16)
        part = jnp.dot(hdn, wd_ref[lo:hi, :], preferred_element_type=F32)
        f = part if f is None else f + part
    return x + 0.5 * _rms(f, gpost_ref[...])


def _ffn_weight_specs(d, d_ff):
    vec = _resident((1, d))
    return [vec, vec, _resident((d, d_ff)), _resident((d, d_ff)), _resident((d_ff, d))]


def _ffn_body(x_ref, gpre_ref, gpost_ref, wg_ref, wu_ref, wd_ref, o_ref):
    o_ref[...] = _ffn_math(x_ref[...], gpre_ref, gpost_ref, wg_ref, wu_ref, wd_ref)


def _ffn(x, gpre, gpost, wg, wu, wd):
    nt, d = x.shape
    tm = _row_tile(nt)
    row = pl.BlockSpec((tm, d), lambda i: (i, 0))
    return pl.pallas_call(
        _ffn_body,
        grid=(nt // tm,),
        in_specs=[row] + _ffn_weight_specs(d, wg.shape[1]),
        out_specs=row,
        out_shape=jax.ShapeDtypeStruct((nt, d), F32),
        compiler_params=_params(),
        name="ffn",
    )(x, gpre, gpost, wg, wu, wd)


class _Layout:
    def __init__(self, shapes):
        self.arr, self.base, self.first, self.length = [], [], [], []
        self.real = [bn * s for bn, s, _ in shapes]
        row = 0
        for a, (bn, s, _) in enumerate(shapes):
            for b in range(bn):
                self.arr.append(a)
                self.base.append(row)
                self.first.append(b * s)
                self.length.append(N_META + s)
                row += N_META + s
        self.n_real = row
        self.tile = _row_tile(row)
        self.n_rows = -(-row // self.tile) * self.tile
        t = self.tile
        for a, (bn, s, _) in enumerate(shapes):
            assert s % t == 0 and s % 16 == 0 and N_META + s >= 2 * t, (s, t)

    def seq_of(self, row):
        return max(s for s in range(len(self.base)) if self.base[s] <= row)

    def gather_tables(self):
        t = self.tile
        n_tiles = self.n_rows // t
        cols = {k: np.zeros((n_tiles,), np.int32)
                for k in ("blk0", "blk1", "src1", "e1", "src2", "e2", "off", "valid")}
        held = [0, 0]
        for i in range(n_tiles):
            u0 = i * t
            starts = [s for s in range(len(self.base)) if u0 <= self.base[s] < u0 + t]
            assert len(starts) <= 1
            off = self.base[starts[0]] - u0 if starts else t
            windows = []
            if off > 0 and u0 < self.n_real:
                windows.append((1, self.seq_of(u0)))
            if starts:
                windows.append((2, starts[0]))
            pairs = {}
            for which, s in windows:
                w = u0 - self.base[s] - N_META + self.first[s]
                blk, e = w // t, w % t
                assert pairs.setdefault(self.arr[s], blk) == blk and e % 16 == 0
                cols["src%d" % which][i], cols["e%d" % which][i] = self.arr[s], e
            for a in range(2):
                held[a] = pairs.get(a, held[a])
                cols["blk%d" % a][i] = held[a]
            cols["off"][i] = off
            cols["valid"][i] = min(max(self.n_real - u0, 0), t)
        return [jnp.asarray(cols[k]) for k in
                ("blk0", "blk1", "src1", "e1", "src2", "e2", "off", "valid")]

    def scatter_table(self):
        t = self.tile
        start = [self.base[s] + N_META + j * t
                 for s in range(len(self.base)) for j in range((self.length[s] - N_META) // t)]
        assert all(r % 16 == 0 and r + t <= self.n_real for r in start)
        return jnp.asarray(start, jnp.int32)


def _ffn_first_body(b0_ref, b1_ref, s1_ref, e1_ref, s2_ref, e2_ref, off_ref, valid_ref,
                    pa_ref, pb_ref, sa_ref, sb_ref, meta_ref,
                    gpre_ref, gpost_ref, wg_ref, wu_ref, wd_ref, o_ref, pair_ref, x_ref):
    i = pl.program_id(0)
    t = o_ref.shape[0]
    pair_ref[0, 0:t, :] = pa_ref[...]
    pair_ref[0, t:2 * t, :] = pb_ref[...]
    pair_ref[1, 0:t, :] = sa_ref[...]
    pair_ref[1, t:2 * t, :] = sb_ref[...]

    def window(src, e):
        return pair_ref[src, pl.ds(pl.multiple_of(e, 16), t), :]

    row = lax.broadcasted_iota(jnp.int32, (t, 1), 0)
    x = jnp.where(row < off_ref[i], window(s1_ref[i], e1_ref[i]), window(s2_ref[i], e2_ref[i]))
    x_ref[...] = jnp.where(row < valid_ref[i], x, 0.0)

    @pl.when(off_ref[i] < t)
    def _():
        x_ref[pl.ds(pl.multiple_of(off_ref[i], 16), N_META), :] = meta_ref[...]

    o_ref[...] = _ffn_math(x_ref[...], gpre_ref, gpost_ref, wg_ref, wu_ref, wd_ref)


def _ffn_first(layout, x_flat, meta, gpre, gpost, wg, wu, wd):
    d = meta.shape[1]
    t = layout.tile
    tables = layout.gather_tables()
    n_blk = [x.shape[0] // t for x in x_flat]

    def block(a, second):
        def index(i, b0, b1, *_):
            return (jnp.clip((b0, b1)[a][i] + second, 0, n_blk[a] - 1), 0)
        return pl.BlockSpec((t, d), index)

    grid_spec = pltpu.PrefetchScalarGridSpec(
        num_scalar_prefetch=len(tables),
        grid=(layout.n_rows // t,),
        in_specs=[block(0, 0), block(0, 1), block(1, 0), block(1, 1), _resident((N_META, d))]
        + _ffn_weight_specs(d, wg.shape[1]),
        out_specs=pl.BlockSpec((t, d), lambda i, *_: (i, 0)),
        scratch_shapes=[pltpu.VMEM((2, 2 * t, d), F32), pltpu.VMEM((t, d), F32)],
    )
    return pl.pallas_call(
        _ffn_first_body,
        grid_spec=grid_spec,
        out_shape=jax.ShapeDtypeStruct((layout.n_rows, d), F32),
        compiler_params=_params(),
        name="ffn_first",
    )(*tables, x_flat[0], x_flat[0], x_flat[1], x_flat[1], meta, gpre, gpost, wg, wu, wd)


def _ffn_last_body(start_ref, h_ref, gpre_ref, gpost_ref, wg_ref, wu_ref, wd_ref,
                   gfin_ref, yp_ref, ys_ref, *, n_first):
    i = pl.program_id(0)
    y = _rms(_ffn_math(h_ref[...], gpre_ref, gpost_ref, wg_ref, wu_ref, wd_ref), gfin_ref[...])

    @pl.when(i < n_first)
    def _():
        yp_ref[...] = y

    @pl.when(i >= n_first)
    def _():
        ys_ref[...] = y


def _ffn_last(layout, h, gpre, gpost, wg, wu, wd, gfinal):
    d = h.shape[1]
    t = layout.tile
    start = layout.scatter_table()
    n_out = [r // t for r in layout.real]
    grid_spec = pltpu.PrefetchScalarGridSpec(
        num_scalar_prefetch=1,
        grid=(n_out[0] + n_out[1],),
        in_specs=[pl.BlockSpec((pl.Element(t), pl.Element(d)),
                               lambda i, start: (pl.multiple_of(start[i], 16), 0))]
        + _ffn_weight_specs(d, wg.shape[1]) + [_resident((1, d))],
        out_specs=[pl.BlockSpec((t, d), lambda i, *_: (jnp.minimum(i, n_out[0] - 1), 0)),
                   pl.BlockSpec((t, d), lambda i, *_: (jnp.maximum(i - n_out[0], 0), 0))],
    )
    return pl.pallas_call(
        functools.partial(_ffn_last_body, n_first=n_out[0]),
        grid_spec=grid_spec,
        out_shape=[jax.ShapeDtypeStruct((r, d), F32) for r in layout.real],
        compiler_params=_params(),
        name="ffn_last",
    )(start, h, gpre, gpost, wg, wu, wd, gfinal)


def _conv_in_body(x_ref, gpre_ref, win_ref, gb_ref, z_ref):
    d = x_ref.shape[1]
    u = _rms(x_ref[...], gpre_ref[...]).astype(BF16)
    p = jnp.dot(u, win_ref[...], preferred_element_type=F32)
    gb_ref[...] = p[:, :d]
    z_ref[...] = p[:, d:2 * d] * p[:, 2 * d:]


def _conv_in(x, gpre, w_in):
    nt, d = x.shape
    tm = _row_tile(nt)
    row = pl.BlockSpec((tm, d), lambda i: (i, 0))
    return pl.pallas_call(
        _conv_in_body,
        grid=(nt // tm,),
        in_specs=[row, _resident((1, d)), _resident((d, 3 * d))],
        out_specs=[row, row],
        out_shape=[jax.ShapeDtypeStruct((nt, d), F32)] * 2,
        compiler_params=_params(),
        name="conv_in",
    )(x, gpre, w_in)


def _conv_out_body(so_ref, eo_ref, z_ref, zp_ref, zn_ref, gb_ref, h_ref, wc_ref, wout_ref,
                   gpost_ref, o_ref):
    i = pl.program_id(0)
    tm = z_ref.shape[0]
    z = z_ref[...]
    row = lax.broadcasted_iota(jnp.int32, (tm, 1), 0)
    zm1 = jnp.where(row == 0, zp_ref[7:8, :], pltpu.roll(z, 1, 0))
    zm1 = jnp.where(row == so_ref[i], 0.0, zm1)
    zp1 = jnp.where(row == tm - 1, zn_ref[0:1, :], pltpu.roll(z, tm - 1, 0))
    zp1 = jnp.where(row == eo_ref[i], 0.0, zp1)
    conv = wc_ref[0:1, :] * zm1 + wc_ref[1:2, :] * z + wc_ref[2:3, :] * zp1
    y = jnp.dot((gb_ref[...] * conv).astype(BF16), wout_ref[...], preferred_element_type=F32)
    o_ref[...] = h_ref[...] + _rms(y, gpost_ref[...])


def _conv_out(z, gb, h, w_conv, w_out, gpost, start_off, end_off):
    nt, d = z.shape
    tm = _row_tile(nt)
    sub = tm // 8
    n_sub = nt // 8
    row = pl.BlockSpec((tm, d), lambda i, *_: (i, 0))
    prev8 = pl.BlockSpec((8, d), lambda i, *_: (jnp.maximum(i * sub - 1, 0), 0))
    next8 = pl.BlockSpec((8, d), lambda i, *_: (jnp.minimum((i + 1) * sub, n_sub - 1), 0))
    grid_spec = pltpu.PrefetchScalarGridSpec(
        num_scalar_prefetch=2,
        grid=(nt // tm,),
        in_specs=[row, prev8, next8, row, row, _resident(w_conv.shape), _resident((d, d)),
                  _resident((1, d))],
        out_specs=row,
    )
    return pl.pallas_call(
        _conv_out_body,
        grid_spec=grid_spec,
        out_shape=jax.ShapeDtypeStruct((nt, d), F32),
        compiler_params=_params(),
        name="conv_out",
    )(start_off, end_off, z, z, z, gb, h, w_conv, w_out, gpost)


def _log_gates(z, lb):
    t = _exp_neg(jnp.abs(z))
    pos = z >= 0.0
    r = 1.0 / (1.0 + t)
    f = jnp.where(pos, 1.0 + lb * t, lb + t) * r
    logf = jnp.maximum(jnp.log(f), NEG_BIG)
    key = (1.0 - lb) * (jnp.where(pos, t, 1.0) * r)
    return logf, key


def _split_hi_lo(x):
    hi = x.astype(BF16)
    lo = (x - hi.astype(F32)).astype(BF16)
    return hi, lo


def _hg_in_body(so_ref, eo_ref, x_ref, gpre_ref, win_ref, logit_ref,
                q_ref, v_ref, g_ref, kf_ref, kb_ref, hlf_ref, hlb_ref, *, layer):
    i = pl.program_id(0)
    tm, d = x_ref.shape
    u = _rms(x_ref[...], gpre_ref[...]).astype(BF16)
    depth = logit_ref.shape[0] // 2
    row = lax.broadcasted_iota(jnp.int32, (tm, 1), 0)

    def proj(j, cols):
        return jnp.dot(u, win_ref[:, j * d + cols.start:j * d + cols.stop],
                       preferred_element_type=F32)

    for cols in (slice(j, min(j + MXU_TILE, d)) for j in range(0, d, MXU_TILE)):
        for direction, (k_ref, hl_ref, off_ref) in enumerate(
                ((kf_ref, hlf_ref, so_ref), (kb_ref, hlb_ref, eo_ref))):
            if direction == 0:
                q_ref[:, cols] = _pack(_silu(proj(0, cols)).astype(BF16))
            else:
                g_ref[:, cols] = _silu(proj(4, cols)).astype(BF16)
            logits = [logit_ref[2 * j + direction:2 * j + direction + 1, cols]
                      for j in range(depth)]
            m = functools.reduce(jnp.maximum, logits)
            e = [jnp.exp(l - m) for l in logits]
            lb = sum(e[1:layer + 1]) / sum(e)
            logf, key = _log_gates(proj(2 + direction, cols), lb)
            logf = jnp.where(row == off_ref[i], NEG_BIG, logf)
            k_ref[:, cols] = _pack(key.astype(BF16))
            hi, lo = _split_hi_lo(logf)
            for c in range(tm // CHUNK):
                hl_ref[2 * c * CHUNK:(2 * c + 1) * CHUNK, cols] = hi[c * CHUNK:(c + 1) * CHUNK]
                hl_ref[(2 * c + 1) * CHUNK:(2 * c + 2) * CHUNK, cols] = lo[c * CHUNK:(c + 1) * CHUNK]
        v_ref[:, cols] = _pack(proj(1, cols).astype(BF16))


def _hg_in(x, gpre, w_in, logits, start_off, end_off, layer):
    nt, d = x.shape
    tm = _row_tile(nt)
    row = pl.BlockSpec((tm, d), lambda i, *_: (i, 0))
    row2 = pl.BlockSpec((2 * tm, d), lambda i, *_: (i, 0))
    words = pl.BlockSpec((_packed_rows(tm), d), lambda i, *_: (i, 0))
    plain = jax.ShapeDtypeStruct((nt, d), BF16)
    packed = jax.ShapeDtypeStruct((_packed_rows(nt), d), jnp.uint32)
    grid_spec = pltpu.PrefetchScalarGridSpec(
        num_scalar_prefetch=2,
        grid=(nt // tm,),
        in_specs=[row, _resident((1, d)), _resident((d, 5 * d)), _resident(logits.shape)],
        out_specs=[words, words, row, words, words, row2, row2],
    )
    return pl.pallas_call(
        functools.partial(_hg_in_body, layer=layer),
        grid_spec=grid_spec,
        out_shape=[packed, packed, plain, packed, packed]
        + [jax.ShapeDtypeStruct((2 * nt, d), BF16)] * 2,
        compiler_params=_params(),
        name="hg_in",
    )(start_off, end_off, x, gpre, w_in, logits)


def _scan_tables():
    c = CHUNK
    w = np.zeros((N_LEVELS + 1, c, c), np.float32)
    for l in range(N_LEVELS):
        m = c >> (l + 1)
        for r in range(c):
            n = (r // (2 * m)) * 2 * m + m - 1
            lo, hi = (r, n) if r <= n else (n, r)
            w[l, r, lo + 1:hi + 1] = 1.0
    for r in range(c):
        w[N_LEVELS, r, :r + 1] = 1.0
    keep = [N_LEVELS] + list(range(N_VPU_LEVELS, N_LEVELS))
    w_f = np.tile(w[keep].reshape(-1, c), (1, 2))
    w_b = np.tile(w[keep][:, ::-1, ::-1].reshape(-1, c), (1, 2))
    lv_f, lv_b = _level_tables()
    return jnp.asarray(w_f, BF16), jnp.asarray(w_b, BF16), jnp.asarray(lv_f), jnp.asarray(lv_b)


def _level_tables():
    c = CHUNK
    level = np.full((c, c), N_LEVELS + 1, np.int32)
    for l in range(N_LEVELS):
        m = c >> (l + 1)
        for t in range(c):
            for s in range(c):
                if t // (2 * m) == s // (2 * m) and (t // m) % 2 == 1 and (s // m) % 2 == 0:
                    level[t, s] = l
    level[np.arange(c), np.arange(c)] = N_LEVELS
    return level, level[::-1, ::-1].copy()


def _reference_rows(level, backward):
    c = CHUNK
    if level == N_LEVELS + 1:
        return [(0, c, 0 if backward else c - 1)]
    m = c >> (level + 1)
    return [(g, g + 2 * m, g + (m if backward else m - 1)) for g in range(0, c, 2 * m)]


def _scan_body(wf_ref, wb_ref, lvf_ref, lvb_ref,
               qf_ref, vf_ref, kf_ref, hlf_ref,
               qb_ref, vb_ref, kb_ref, hlb_ref, nhlf_ref, nhlb_ref,
               of_ref, ob_ref, s_ref, e_ref, dec_ref, *, n_heads):
    tb, d = of_ref.shape
    n_chunks = tb // CHUNK
    c = CHUNK

    @pl.when(pl.program_id(0) == 0)
    def _():
        s_ref[...] = jnp.zeros_like(s_ref)

    dirs = (
        (0, wf_ref, lvf_ref, qf_ref, vf_ref, kf_ref, hlf_ref, of_ref, c - 1),
        (1, wb_ref, lvb_ref, qb_ref, vb_ref, kb_ref, hlb_ref, ob_ref, 0),
    )
    masks = []
    for _, _, lv_ref, *_ in dirs:
        lv = lv_ref[...]
        masks.append([[lv[p:p + SUBLANES] == l for p in range(0, c, SUBLANES)]
                      for l in range(N_LEVELS + 1)])
    owned = [[[bool((lv_np[p:p + SUBLANES] == l).any()) for p in range(0, c, SUBLANES)]
              for l in range(N_LEVELS + 1)] for lv_np in _level_tables()]
    parity = lax.broadcasted_iota(jnp.int32, (c, 1), 0) & 1
    query_side = [parity == 1, parity == 0]
    shifts = [1, c - 1]
    contract_last = (((1,), (1,)), ((), ()))
    contract_rows = (((0,), (0,)), ((), ()))
    heads = [slice(h * HEAD_DIM, (h + 1) * HEAD_DIM) for h in range(n_heads)]
    lane_blocks = [slice(j, min(j + MXU_TILE, d)) for j in range(0, d, MXU_TILE)]

    def chunk_start(di, ci, rows_per_chunk):
        start = (ci if di == 0 else n_chunks - 1 - ci) * rows_per_chunk
        return start if isinstance(start, int) else pl.multiple_of(start, rows_per_chunk)

    def store_factor(buf, di, group, lanes, value):
        for h in range(lanes.start // HEAD_DIM, lanes.stop // HEAD_DIM):
            lo = h * HEAD_DIM - lanes.start
            e_ref[buf, di, h, _packed_rows(group * c):_packed_rows((group + 1) * c), :] = (
                _pack(value[:, lo:lo + HEAD_DIM].astype(BF16)))

    def coarse_exponent(b, level, backward):
        tiles = []
        for lo, hi, n in _reference_rows(level, backward):
            ref = b[n:n + 1, :]
            for p in range(lo, hi, SUBLANES):
                rows = b[p:p + SUBLANES]
                before = p + SUBLANES - 1 < n if backward else p + SUBLANES - 1 <= n
                after = p >= n if backward else p > n
                if before:
                    tiles.append(rows - ref if backward else ref - rows)
                elif after:
                    tiles.append(ref - rows if backward else rows - ref)
                else:
                    tiles.append(-jnp.abs(rows - ref))
        return jnp.concatenate(tiles, axis=0)

    def decay_factors(buf, ci):
        for di, w_ref, _, _, _, _, hl_ref, _, total_row in dirs:
            if ci is None:
                hl_ref, start = (nhlf_ref, nhlb_ref)[di], 0
            else:
                start = chunk_start(di, ci, 2 * c)
            for lanes in lane_blocks:
                x = jnp.dot(w_ref[...], hl_ref[pl.ds(start, 2 * c), lanes],
                            preferred_element_type=F32)
                b = x[0:c]
                e_b = jnp.exp(b)
                store_factor(buf, di, N_LEVELS, lanes, e_b)
                dec_ref[buf, di, :, lanes] = e_b[total_row:total_row + 1, :]
                for j, l in enumerate(range(N_VPU_LEVELS, N_LEVELS)):
                    store_factor(buf, di, l, lanes, jnp.exp(x[(j + 1) * c:(j + 2) * c]))
                for l in list(range(N_VPU_LEVELS)) + [N_LEVELS + 1]:
                    store_factor(buf, di, l, lanes, jnp.exp(coarse_exponent(b, l, di == 1)))

    def recurrence(buf, ci):
        def factor(di, group, h):
            return _unpack(
                e_ref[buf, di, h, _packed_rows(group * c):_packed_rows((group + 1) * c), :])

        inter, scores, own, near = {}, {}, {}, {}
        rows = [pl.ds(chunk_start(di, ci, c), c) for di in range(2)]
        pc = _packed_rows(c)
        words = [pl.ds(chunk_start(di, ci, pc), pc) for di in range(2)]
        for di, _, _, q_ref, v_ref, k_ref, _, _, _ in dirs:
            for h, sl in enumerate(heads):
                q, k = _unpack(q_ref[words[di], sl]), _unpack(k_ref[words[di], sl])
                q_in = q * factor(di, N_LEVELS, h)
                k_out = k * factor(di, N_LEVELS + 1, h)
                st = s_ref[di * n_heads + h]
                inter[di, h] = lax.dot_general(q_in, st.astype(BF16), contract_last,
                                               preferred_element_type=F32)
                ds = lax.dot_general(_unpack(v_ref[words[di], sl]), k_out, contract_rows,
                                     preferred_element_type=F32)
                s_ref[di * n_heads + h] = st * dec_ref[buf, di, :, sl] + ds
                own[di, h] = jnp.sum(q.astype(F32) * k.astype(F32), axis=-1, keepdims=True)
                f = factor(di, N_LEVELS - 1, h)
                pair = jnp.sum((q * f).astype(F32) * pltpu.roll((k * f).astype(F32), shifts[di], 0),
                               axis=-1, keepdims=True)
                near[di, h] = jnp.where(query_side[di], pair, 0.0)
                sc = [jnp.zeros((SUBLANES, c), F32)] * (c // SUBLANES)
                for l in range(N_LEVELS - 1):
                    f = factor(di, l, h)
                    a = lax.dot_general(q * f, k * f, contract_last, preferred_element_type=F32)
                    for p in range(c // SUBLANES):
                        if owned[di][l][p]:
                            sc[p] = jnp.where(masks[di][l][p],
                                              a[p * SUBLANES:(p + 1) * SUBLANES], sc[p])
                scores[di, h] = jnp.concatenate(sc, axis=0).astype(BF16)
        for di, _, _, _, v_ref, _, _, o_ref, _ in dirs:
            for h, sl in enumerate(heads):
                v = _unpack(v_ref[words[di], sl])
                intra = jnp.dot(scores[di, h], v, preferred_element_type=F32)
                v32 = v.astype(F32)
                o_ref[rows[di], sl] = (inter[di, h] + intra + own[di, h] * v32
                                       + near[di, h] * pltpu.roll(v32, shifts[di], 0))

    @pl.when(pl.program_id(0) == 0)
    def _():
        decay_factors(0, 0)

    def step_pair(j, carry):
        for half in range(2):
            ci = 2 * j + half
            decay_factors(1 - half, ci + 1)
            recurrence(half, ci)
        return carry

    assert n_chunks % 2 == 0
    lax.fori_loop(0, n_chunks // 2 - 1, step_pair, 0)
    decay_factors(1, n_chunks - 1)
    recurrence(0, n_chunks - 2)
    decay_factors(0, None)
    recurrence(1, n_chunks - 1)


def _scan(q, v, kf, hlf, kb, hlb):
    nt, d = hlf.shape[0] // 2, hlf.shape[1]
    tb = _row_tile(nt)
    nb = nt // tb
    n_heads = d // HEAD_DIM
    w_f, w_b, lv_f, lv_b = _scan_tables()
    fwd = pl.BlockSpec((tb, d), lambda i: (i, 0))
    bwd = pl.BlockSpec((tb, d), lambda i: (nb - 1 - i, 0))
    fwd2 = pl.BlockSpec((2 * tb, d), lambda i: (i, 0))
    bwd2 = pl.BlockSpec((2 * tb, d), lambda i: (nb - 1 - i, 0))
    fwdw = pl.BlockSpec((_packed_rows(tb), d), lambda i: (i, 0))
    bwdw = pl.BlockSpec((_packed_rows(tb), d), lambda i: (nb - 1 - i, 0))
    n_chunks = tb // CHUNK
    fwd_next = pl.BlockSpec((2 * CHUNK, d),
                            lambda i: (jnp.minimum((i + 1) * n_chunks, nb * n_chunks - 1), 0))
    bwd_next = pl.BlockSpec((2 * CHUNK, d),
                            lambda i: (jnp.maximum((nb - 1 - i) * n_chunks - 1, 0), 0))
    return pl.pallas_call(
        functools.partial(_scan_body, n_heads=n_heads),
        grid=(nb,),
        in_specs=[_resident(w_f.shape), _resident(w_b.shape), _resident(lv_f.shape),
                  _resident(lv_b.shape), fwdw, fwdw, fwdw, fwd2, bwdw, bwdw, bwdw, bwd2,
                  fwd_next, bwd_next],
        out_specs=[fwd, bwd],
        out_shape=[jax.ShapeDtypeStruct((nt, d), F32)] * 2,
        scratch_shapes=[
            pltpu.VMEM((2 * n_heads, HEAD_DIM, HEAD_DIM), F32),
            pltpu.VMEM((2, 2, n_heads, _packed_rows((N_LEVELS + 2) * CHUNK), HEAD_DIM), jnp.uint32),
            pltpu.VMEM((2, 2, 1, d), F32),
        ],
        compiler_params=_params(),
        name="hg_scan",
    )(w_f, w_b, lv_f, lv_b, q, v, kf, hlf, q, v, kb, hlb, hlf, hlb)


def _hg_out_body(of_ref, ob_ref, g_ref, h_ref, gn_ref, wout_ref, gpost_ref, o_ref):
    d = h_ref.shape[1]
    gated = (of_ref[...] + ob_ref[...]) * g_ref[...].astype(F32)
    parts = []
    for h in range(d // HEAD_DIM):
        gh = gated[:, h * HEAD_DIM:(h + 1) * HEAD_DIM]
        parts.append(gh * lax.rsqrt(jnp.mean(gh * gh, axis=-1, keepdims=True) + EPS))
    normed = jnp.concatenate(parts, axis=-1) * gn_ref[...]
    y = jnp.dot(normed.astype(BF16), wout_ref[...], preferred_element_type=F32)
    o_ref[...] = h_ref[...] + _rms(y, gpost_ref[...])


def _hg_out(o_f, o_b, g, h, gn, w_out, gpost):
    nt, d = h.shape
    tm = _row_tile(nt)
    row = pl.BlockSpec((tm, d), lambda i: (i, 0))
    vec = _resident((1, d))
    return pl.pallas_call(
        _hg_out_body,
        grid=(nt // tm,),
        in_specs=[row, row, row, row, vec, _resident((d, d)), vec],
        out_specs=row,
        out_shape=jax.ShapeDtypeStruct((nt, d), F32),
        compiler_params=_params(),
        name="hg_out",
    )(o_f, o_b, g, h, gn, w_out, gpost)


def _boundary_tables(seq_lens, n_rows, tile):
    n_tiles = n_rows // tile
    start = np.full((n_tiles,), -1, np.int32)
    end = np.full((n_tiles,), -1, np.int32)
    base = 0
    for length in seq_lens:
        assert length >= 2 * tile, "at most one sequence boundary per row tile"
        start[base // tile] = base % tile
        last = base + length - 1
        end[last // tile] = last % tile
        base += length
    return jnp.asarray(start), jnp.asarray(end)


def kernel(x_prompt, x_sample, meta_tokens, norm_pre, norm_post, ffn_w_gate, ffn_w_up, ffn_w_down,
           sc_w_in, sc_conv, sc_w_out, hg_w_in, hg_lb_logits, hg_gn, hg_w_out, final_norm):
    d = x_prompt.shape[-1]
    depth = norm_pre.shape[0]
    assert d % HEAD_DIM == 0

    layout = _Layout([x_prompt.shape, x_sample.shape])
    x_flat = [x.reshape(-1, d).astype(F32) for x in (x_prompt, x_sample)]
    start_off, end_off = _boundary_tables(layout.length, layout.n_rows, layout.tile)

    vec = lambda g: g.reshape(1, d).astype(F32)
    w16 = lambda w: w.astype(BF16)
    logits = hg_lb_logits.reshape(2 * depth, -1).astype(F32)

    h = None
    for i in range(depth):
        ffn1 = (vec(norm_pre[i, 0]), vec(norm_post[i, 0]),
                w16(ffn_w_gate[i, 0]), w16(ffn_w_up[i, 0]), w16(ffn_w_down[i, 0]))
        if i == 0:
            h = _ffn_first(layout, x_flat, meta_tokens.astype(F32), *ffn1)
        else:
            h = _ffn(h, *ffn1)
        j = i // 2
        if i % 2 == 0:
            gb, z = _conv_in(h, vec(norm_pre[i, 1]), w16(sc_w_in[j]))
            h = _conv_out(z, gb, h, sc_conv[j].astype(F32), w16(sc_w_out[j]), vec(norm_post[i, 1]),
                          start_off, end_off)
        else:
            q, v, g, kf, kb, hlf, hlb = _hg_in(
                h, vec(norm_pre[i, 1]), w16(hg_w_in[j]), logits, start_off, end_off, layer=i)
            o_f, o_b = _scan(q, v, kf, hlf, kb, hlb)
            h = _hg_out(o_f, o_b, g, h, vec(hg_gn[j]), w16(hg_w_out[j]), vec(norm_post[i, 1]))
        ffn2 = (vec(norm_pre[i, 2]), vec(norm_post[i, 2]),
                w16(ffn_w_gate[i, 1]), w16(ffn_w_up[i, 1]), w16(ffn_w_down[i, 1]))
        if i < depth - 1:
            h = _ffn(h, *ffn2)
    y_prompt, y_sample = _ffn_last(layout, h, *ffn2, vec(final_norm))
    return y_prompt.reshape(x_prompt.shape), y_sample.reshape(x_sample.shape)
```

```python
import functools

import numpy as np
import jax
import jax.numpy as jnp
from jax import lax
from jax.experimental import pallas as pl
from jax.experimental.pallas import tpu as pltpu

EPS = 1e-6
N_META = 16
HEAD_DIM = 128
CHUNK = 64
N_LEVELS = 6
N_VPU_LEVELS = 4
NEG_BIG = -2000.0
LOG2E = 1.4426950408889634
MXU_TILE = 256
SUBLANES = 8
VMEM_LIMIT_BYTES = 56 * 1024 * 1024

F32 = jnp.float32
BF16 = jnp.bfloat16


def _rms(x, g):
    return x * lax.rsqrt(jnp.mean(x * x, axis=-1, keepdims=True) + EPS) * g


def _exp_neg(x):
    return jnp.exp2(x * (-LOG2E))


def _silu(x):
    return x / (1.0 + _exp_neg(x))


def _pack(x):
    return pltpu.bitcast(x, jnp.uint32)


def _unpack(x):
    return pltpu.bitcast(x, BF16)


def _packed_rows(rows):
    return rows * jnp.dtype(BF16).itemsize // 4


def _row_tile(n_rows):
    return 512 if n_rows >= 4096 else 128


def _resident(shape):
    nd = len(shape)
    return pl.BlockSpec(shape, lambda i, *_: (0,) * nd, pipeline_mode=pl.Buffered(1))


def _params():
    return pltpu.CompilerParams(dimension_semantics=("arbitrary",),
                                vmem_limit_bytes=VMEM_LIMIT_BYTES)


def _ffn_splits(d_ff):
    if d_ff % MXU_TILE or d_ff < 4 * MXU_TILE:
        return ((0, d_ff),)
    step = 1 * MXU_TILE
    return tuple((lo, min(lo + step, d_ff)) for lo in range(0, d_ff, step))


def _ffn_math(x, gpre_ref, gpost_ref, wg_ref, wu_ref, wd_ref):
    u = _rms(x, gpre_ref[...]).astype(BF16)
    f = None
    for lo, hi in _ffn_splits(wg_ref.shape[1]):
        a = jnp.dot(u, wg_ref[:, lo:hi], preferred_element_type=F32)
        b = jnp.dot(u, wu_ref[:, lo:hi], preferred_element_type=F32)
        hdn = (_silu(a) * b).astype(BF16)
        part = jnp.dot(hdn, wd_ref[lo:hi, :], preferred_element_type=F32)
        f = part if f is None else f + part
    return x + 0.5 * _rms(f, gpost_ref[...])


def _ffn_weight_specs(d, d_ff):
    vec = _resident((1, d))
    return [vec, vec, _resident((d, d_ff)), _resident((d, d_ff)), _resident((d_ff, d))]


def _ffn_body(x_ref, gpre_ref, gpost_ref, wg_ref, wu_ref, wd_ref, o_ref):
    o_ref[...] = _ffn_math(x_ref[...], gpre_ref, gpost_ref, wg_ref, wu_ref, wd_ref)


def _ffn(x, gpre, gpost, wg, wu, wd):
    nt, d = x.shape
    tm = _row_tile(nt)
    row = pl.BlockSpec((tm, d), lambda i: (i, 0))
    return pl.pallas_call(
        _ffn_body,
        grid=(nt // tm,),
        in_specs=[row] + _ffn_weight_specs(d, wg.shape[1]),
        out_specs=row,
        out_shape=jax.ShapeDtypeStruct((nt, d), F32),
        compiler_params=_params(),
        name="ffn",
    )(x, gpre, gpost, wg, wu, wd)


class _Layout:
    def __init__(self, shapes):
        self.arr, self.base, self.first, self.length = [], [], [], []
        self.real = [bn * s for bn, s, _ in shapes]
        row = 0
        for a, (bn, s, _) in enumerate(shapes):
            for b in range(bn):
                self.arr.append(a)
                self.base.append(row)
                self.first.append(b * s)
                self.length.append(N_META + s)
                row += N_META + s
        self.n_real = row
        self.tile = _row_tile(row)
        self.n_rows = -(-row // self.tile) * self.tile
        t = self.tile
        for a, (bn, s, _) in enumerate(shapes):
            assert s % t == 0 and s % 16 == 0 and N_META + s >= 2 * t, (s, t)

    def seq_of(self, row):
        return max(s for s in range(len(self.base)) if self.base[s] <= row)

    def gather_tables(self):
        t = self.tile
        n_tiles = self.n_rows // t
        cols = {k: np.zeros((n_tiles,), np.int32)
                for k in ("blk0", "blk1", "src1", "e1", "src2", "e2", "off", "valid")}
        held = [0, 0]
        for i in range(n_tiles):
            u0 = i * t
            starts = [s for s in range(len(self.base)) if u0 <= self.base[s] < u0 + t]
            assert len(starts) <= 1
            off = self.base[starts[0]] - u0 if starts else t
            windows = []
            if off > 0 and u0 < self.n_real:
                windows.append((1, self.seq_of(u0)))
            if starts:
                windows.append((2, starts[0]))
            pairs = {}
            for which, s in windows:
                w = u0 - self.base[s] - N_META + self.first[s]
                blk, e = w // t, w % t
                assert pairs.setdefault(self.arr[s], blk) == blk and e % 16 == 0
                cols["src%d" % which][i], cols["e%d" % which][i] = self.arr[s], e
            for a in range(2):
                held[a] = pairs.get(a, held[a])
                cols["blk%d" % a][i] = held[a]
            cols["off"][i] = off
            cols["valid"][i] = min(max(self.n_real - u0, 0), t)
        return [jnp.asarray(cols[k]) for k in
                ("blk0", "blk1", "src1", "e1", "src2", "e2", "off", "valid")]

    def scatter_table(self):
        t = self.tile
        start = [self.base[s] + N_META + j * t
                 for s in range(len(self.base)) for j in range((self.length[s] - N_META) // t)]
        assert all(r % 16 == 0 and r + t <= self.n_real for r in start)
        return jnp.asarray(start, jnp.int32)


def _ffn_first_body(b0_ref, b1_ref, s1_ref, e1_ref, s2_ref, e2_ref, off_ref, valid_ref,
                    pa_ref, pb_ref, sa_ref, sb_ref, meta_ref,
                    gpre_ref, gpost_ref, wg_ref, wu_ref, wd_ref, o_ref, pair_ref, x_ref):
    i = pl.program_id(0)
    t = o_ref.shape[0]
    pair_ref[0, 0:t, :] = pa_ref[...]
    pair_ref[0, t:2 * t, :] = pb_ref[...]
    pair_ref[1, 0:t, :] = sa_ref[...]
    pair_ref[1, t:2 * t, :] = sb_ref[...]

    def window(src, e):
        return pair_ref[src, pl.ds(pl.multiple_of(e, 16), t), :]

    row = lax.broadcasted_iota(jnp.int32, (t, 1), 0)
    x = jnp.where(row < off_ref[i], window(s1_ref[i], e1_ref[i]), window(s2_ref[i], e2_ref[i]))
    x_ref[...] = jnp.where(row < valid_ref[i], x, 0.0)

    @pl.when(off_ref[i] < t)
    def _():
        x_ref[pl.ds(pl.multiple_of(off_ref[i], 16), N_META), :] = meta_ref[...]

    o_ref[...] = _ffn_math(x_ref[...], gpre_ref, gpost_ref, wg_ref, wu_ref, wd_ref)


def _ffn_first(layout, x_flat, meta, gpre, gpost, wg, wu, wd):
    d = meta.shape[1]
    t = layout.tile
    tables = layout.gather_tables()
    n_blk = [x.shape[0] // t for x in x_flat]

    def block(a, second):
        def index(i, b0, b1, *_):
            return (jnp.clip((b0, b1)[a][i] + second, 0, n_blk[a] - 1), 0)
        return pl.BlockSpec((t, d), index)

    grid_spec = pltpu.PrefetchScalarGridSpec(
        num_scalar_prefetch=len(tables),
        grid=(layout.n_rows // t,),
        in_specs=[block(0, 0), block(0, 1), block(1, 0), block(1, 1), _resident((N_META, d))]
        + _ffn_weight_specs(d, wg.shape[1]),
        out_specs=pl.BlockSpec((t, d), lambda i, *_: (i, 0)),
        scratch_shapes=[pltpu.VMEM((2, 2 * t, d), F32), pltpu.VMEM((t, d), F32)],
    )
    return pl.pallas_call(
        _ffn_first_body,
        grid_spec=grid_spec,
        out_shape=jax.ShapeDtypeStruct((layout.n_rows, d), F32),
        compiler_params=_params(),
        name="ffn_first",
    )(*tables, x_flat[0], x_flat[0], x_flat[1], x_flat[1], meta, gpre, gpost, wg, wu, wd)


def _ffn_last_body(start_ref, h_ref, gpre_ref, gpost_ref, wg_ref, wu_ref, wd_ref,
                   gfin_ref, yp_ref, ys_ref, *, n_first):
    i = pl.program_id(0)
    y = _rms(_ffn_math(h_ref[...], gpre_ref, gpost_ref, wg_ref, wu_ref, wd_ref), gfin_ref[...])

    @pl.when(i < n_first)
    def _():
        yp_ref[...] = y

    @pl.when(i >= n_first)
    def _():
        ys_ref[...] = y


def _ffn_last(layout, h, gpre, gpost, wg, wu, wd, gfinal):
    d = h.shape[1]
    t = layout.tile
    start = layout.scatter_table()
    n_out = [r // t for r in layout.real]
    grid_spec = pltpu.PrefetchScalarGridSpec(
        num_scalar_prefetch=1,
        grid=(n_out[0] + n_out[1],),
        in_specs=[pl.BlockSpec((pl.Element(t), pl.Element(d)),
                               lambda i, start: (pl.multiple_of(start[i], 16), 0))]
        + _ffn_weight_specs(d, wg.shape[1]) + [_resident((1, d))],
        out_specs=[pl.BlockSpec((t, d), lambda i, *_: (jnp.minimum(i, n_out[0] - 1), 0)),
                   pl.BlockSpec((t, d), lambda i, *_: (jnp.maximum(i - n_out[0], 0), 0))],
    )
    return pl.pallas_call(
        functools.partial(_ffn_last_body, n_first=n_out[0]),
        grid_spec=grid_spec,
        out_shape=[jax.ShapeDtypeStruct((r, d), F32) for r in layout.real],
        compiler_params=_params(),
        name="ffn_last",
    )(start, h, gpre, gpost, wg, wu, wd, gfinal)


def _conv_in_body(x_ref, gpre_ref, win_ref, gb_ref, z_ref):
    d = x_ref.shape[1]
    u = _rms(x_ref[...], gpre_ref[...]).astype(BF16)
    p = jnp.dot(u, win_ref[...], preferred_element_type=F32)
    gb_ref[...] = p[:, :d]
    z_ref[...] = p[:, d:2 * d] * p[:, 2 * d:]


def _conv_in(x, gpre, w_in):
    nt, d = x.shape
    tm = _row_tile(nt)
    row = pl.BlockSpec((tm, d), lambda i: (i, 0))
    return pl.pallas_call(
        _conv_in_body,
        grid=(nt // tm,),
        in_specs=[row, _resident((1, d)), _resident((d, 3 * d))],
        out_specs=[row, row],
        out_shape=[jax.ShapeDtypeStruct((nt, d), F32)] * 2,
        compiler_params=_params(),
        name="conv_in",
    )(x, gpre, w_in)


def _conv_out_body(so_ref, eo_ref, z_ref, zp_ref, zn_ref, gb_ref, h_ref, wc_ref, wout_ref,
                   gpost_ref, o_ref):
    i = pl.program_id(0)
    tm = z_ref.shape[0]
    z = z_ref[...]
    row = lax.broadcasted_iota(jnp.int32, (tm, 1), 0)
    zm1 = jnp.where(row == 0, zp_ref[7:8, :], pltpu.roll(z, 1, 0))
    zm1 = jnp.where(row == so_ref[i], 0.0, zm1)
    zp1 = jnp.where(row == tm - 1, zn_ref[0:1, :], pltpu.roll(z, tm - 1, 0))
    zp1 = jnp.where(row == eo_ref[i], 0.0, zp1)
    conv = wc_ref[0:1, :] * zm1 + wc_ref[1:2, :] * z + wc_ref[2:3, :] * zp1
    y = jnp.dot((gb_ref[...] * conv).astype(BF16), wout_ref[...], preferred_element_type=F32)
    o_ref[...] = h_ref[...] + _rms(y, gpost_ref[...])


def _conv_out(z, gb, h, w_conv, w_out, gpost, start_off, end_off):
    nt, d = z.shape
    tm = _row_tile(nt)
    sub = tm // 8
    n_sub = nt // 8
    row = pl.BlockSpec((tm, d), lambda i, *_: (i, 0))
    prev8 = pl.BlockSpec((8, d), lambda i, *_: (jnp.maximum(i * sub - 1, 0), 0))
    next8 = pl.BlockSpec((8, d), lambda i, *_: (jnp.minimum((i + 1) * sub, n_sub - 1), 0))
    grid_spec = pltpu.PrefetchScalarGridSpec(
        num_scalar_prefetch=2,
        grid=(nt // tm,),
        in_specs=[row, prev8, next8, row, row, _resident(w_conv.shape), _resident((d, d)),
                  _resident((1, d))],
        out_specs=row,
    )
    return pl.pallas_call(
        _conv_out_body,
        grid_spec=grid_spec,
        out_shape=jax.ShapeDtypeStruct((nt, d), F32),
        compiler_params=_params(),
        name="conv_out",
    )(start_off, end_off, z, z, z, gb, h, w_conv, w_out, gpost)


def _log_gates(z, lb):
    t = _exp_neg(jnp.abs(z))
    pos = z >= 0.0
    r = 1.0 / (1.0 + t)
    f = jnp.where(pos, 1.0 + lb * t, lb + t) * r
    log2f = jnp.maximum(jnp.log(f) * LOG2E, NEG_BIG)
    key = (1.0 - lb) * (jnp.where(pos, t, 1.0) * r)
    return log2f, key


def _split_hi_lo(x):
    hi = x.astype(BF16)
    lo = (x - hi.astype(F32)).astype(BF16)
    return hi, lo


def _hg_in_body(so_ref, eo_ref, x_ref, gpre_ref, win_ref, logit_ref,
                q_ref, v_ref, g_ref, kf_ref, kb_ref, hlf_ref, hlb_ref, *, layer):
    i = pl.program_id(0)
    tm, d = x_ref.shape
    u = _rms(x_ref[...], gpre_ref[...]).astype(BF16)
    depth = logit_ref.shape[0] // 2
    row = lax.broadcasted_iota(jnp.int32, (tm, 1), 0)

    def proj(j, cols):
        return jnp.dot(u, win_ref[:, j * d + cols.start:j * d + cols.stop],
                       preferred_element_type=F32)

    for cols in (slice(j, min(j + MXU_TILE, d)) for j in range(0, d, MXU_TILE)):
        for direction, (k_ref, hl_ref, off_ref) in enumerate(
                ((kf_ref, hlf_ref, so_ref), (kb_ref, hlb_ref, eo_ref))):
            if direction == 0:
                q_ref[:, cols] = _pack(_silu(proj(0, cols)).astype(BF16))
            else:
                g_ref[:, cols] = _silu(proj(4, cols)).astype(BF16)
            logits = [logit_ref[2 * j + direction:2 * j + direction + 1, cols]
                      for j in range(depth)]
            m = functools.reduce(jnp.maximum, logits)
            e = [jnp.exp(l - m) for l in logits]
            lb = sum(e[1:layer + 1]) / sum(e)
            logf, key = _log_gates(proj(2 + direction, cols), lb)
            logf = jnp.where(row == off_ref[i], NEG_BIG, logf)
            k_ref[:, cols] = _pack(key.astype(BF16))
            hi, lo = _split_hi_lo(logf)
            for c in range(tm // CHUNK):
                pc = _packed_rows(CHUNK)
                hl_ref[2 * c * pc:(2 * c + 1) * pc, cols] = _pack(hi[c * CHUNK:(c + 1) * CHUNK])
                hl_ref[(2 * c + 1) * pc:(2 * c + 2) * pc, cols] = _pack(lo[c * CHUNK:(c + 1) * CHUNK])
        v_ref[:, cols] = _pack(proj(1, cols).astype(BF16))


def _hg_in(x, gpre, w_in, logits, start_off, end_off, layer):
    nt, d = x.shape
    tm = _row_tile(nt)
    row = pl.BlockSpec((tm, d), lambda i, *_: (i, 0))
    row2 = pl.BlockSpec((_packed_rows(2 * tm), d), lambda i, *_: (i, 0))
    words = pl.BlockSpec((_packed_rows(tm), d), lambda i, *_: (i, 0))
    plain = jax.ShapeDtypeStruct((nt, d), BF16)
    packed = jax.ShapeDtypeStruct((_packed_rows(nt), d), jnp.uint32)
    grid_spec = pltpu.PrefetchScalarGridSpec(
        num_scalar_prefetch=2,
        grid=(nt // tm,),
        in_specs=[row, _resident((1, d)), _resident((d, 5 * d)), _resident(logits.shape)],
        out_specs=[words, words, row, words, words, row2, row2],
    )
    return pl.pallas_call(
        functools.partial(_hg_in_body, layer=layer),
        grid_spec=grid_spec,
        out_shape=[packed, packed, plain, packed, packed]
        + [jax.ShapeDtypeStruct((_packed_rows(2 * nt), d), jnp.uint32)] * 2,
        compiler_params=_params(),
        name="hg_in",
    )(start_off, end_off, x, gpre, w_in, logits)


def _scan_tables():
    c = CHUNK
    w = np.zeros((N_LEVELS + 1, c, c), np.float32)
    for l in range(N_LEVELS):
        m = c >> (l + 1)
        for r in range(c):
            n = (r // (2 * m)) * 2 * m + m - 1
            lo, hi = (r, n) if r <= n else (n, r)
            w[l, r, lo + 1:hi + 1] = 1.0
    for r in range(c):
        w[N_LEVELS, r, :r + 1] = 1.0
    keep = [N_LEVELS] + list(range(N_VPU_LEVELS, N_LEVELS))
    w_f = np.tile(w[keep].reshape(-1, c), (1, 2))
    w_b = np.tile(w[keep][:, ::-1, ::-1].reshape(-1, c), (1, 2))
    lv_f, lv_b = _level_tables()
    return jnp.asarray(w_f, BF16), jnp.asarray(w_b, BF16), jnp.asarray(lv_f), jnp.asarray(lv_b)


def _level_tables():
    c = CHUNK
    level = np.full((c, c), N_LEVELS + 1, np.int32)
    for l in range(N_LEVELS):
        m = c >> (l + 1)
        for t in range(c):
            for s in range(c):
                if t // (2 * m) == s // (2 * m) and (t // m) % 2 == 1 and (s // m) % 2 == 0:
                    level[t, s] = l
    level[np.arange(c), np.arange(c)] = N_LEVELS
    return level, level[::-1, ::-1].copy()


def _reference_rows(level, backward):
    c = CHUNK
    if level == N_LEVELS + 1:
        return [(0, c, 0 if backward else c - 1)]
    m = c >> (level + 1)
    return [(g, g + 2 * m, g + (m if backward else m - 1)) for g in range(0, c, 2 * m)]


def _scan_body(wf_ref, wb_ref, lvf_ref, lvb_ref,
               qf_ref, vf_ref, kf_ref, hlf_ref,
               qb_ref, vb_ref, kb_ref, hlb_ref, nhlf_ref, nhlb_ref,
               of_ref, ob_ref, s_ref, e_ref, dec_ref, *, n_heads):
    tb, d = of_ref.shape
    n_chunks = tb // CHUNK
    c = CHUNK

    @pl.when(pl.program_id(0) == 0)
    def _():
        s_ref[...] = jnp.zeros_like(s_ref)

    dirs = (
        (0, wf_ref, lvf_ref, qf_ref, vf_ref, kf_ref, hlf_ref, of_ref, c - 1),
        (1, wb_ref, lvb_ref, qb_ref, vb_ref, kb_ref, hlb_ref, ob_ref, 0),
    )
    masks = []
    for _, _, lv_ref, *_ in dirs:
        lv = lv_ref[...]
        masks.append([[lv[p:p + SUBLANES] == l for p in range(0, c, SUBLANES)]
                      for l in range(N_LEVELS + 1)])
    owned = [[[bool((lv_np[p:p + SUBLANES] == l).any()) for p in range(0, c, SUBLANES)]
              for l in range(N_LEVELS + 1)] for lv_np in _level_tables()]
    parity = lax.broadcasted_iota(jnp.int32, (c, 1), 0) & 1
    query_side = [parity == 1, parity == 0]
    shifts = [1, c - 1]
    contract_last = (((1,), (1,)), ((), ()))
    contract_rows = (((0,), (0,)), ((), ()))
    heads = [slice(h * HEAD_DIM, (h + 1) * HEAD_DIM) for h in range(n_heads)]
    lane_blocks = [slice(j, min(j + MXU_TILE, d)) for j in range(0, d, MXU_TILE)]

    def chunk_start(di, ci, rows_per_chunk):
        start = (ci if di == 0 else n_chunks - 1 - ci) * rows_per_chunk
        return start if isinstance(start, int) else pl.multiple_of(start, rows_per_chunk)

    def store_factor(buf, di, group, lanes, value):
        for h in range(lanes.start // HEAD_DIM, lanes.stop // HEAD_DIM):
            lo = h * HEAD_DIM - lanes.start
            e_ref[buf, di, h, _packed_rows(group * c):_packed_rows((group + 1) * c), :] = (
                _pack(value[:, lo:lo + HEAD_DIM].astype(BF16)))

    def coarse_exponent(b, level, backward):
        tiles = []
        for lo, hi, n in _reference_rows(level, backward):
            ref = b[n:n + 1, :]
            for p in range(lo, hi, SUBLANES):
                rows = b[p:p + SUBLANES]
                before = p + SUBLANES - 1 < n if backward else p + SUBLANES - 1 <= n
                after = p >= n if backward else p > n
                if before:
                    tiles.append(rows - ref if backward else ref - rows)
                elif after:
                    tiles.append(ref - rows if backward else rows - ref)
                else:
                    tiles.append(-jnp.abs(rows - ref))
        return jnp.concatenate(tiles, axis=0)

    def decay_factors(buf, ci):
        for di, w_ref, _, _, _, _, hl_ref, _, total_row in dirs:
            pc2 = _packed_rows(2 * c)
            if ci is None:
                hl_ref, start = (nhlf_ref, nhlb_ref)[di], 0
            else:
                start = chunk_start(di, ci, pc2)
            for lanes in lane_blocks:
                x = jnp.dot(w_ref[...], _unpack(hl_ref[pl.ds(start, pc2), lanes]),
                            preferred_element_type=F32)
                b = x[0:c]
                e_b = jnp.exp2(b)
                store_factor(buf, di, N_LEVELS, lanes, e_b)
                dec_ref[buf, di, :, lanes] = e_b[total_row:total_row + 1, :]
                for j, l in enumerate(range(N_VPU_LEVELS, N_LEVELS)):
                    store_factor(buf, di, l, lanes, jnp.exp2(x[(j + 1) * c:(j + 2) * c]))
                for l in list(range(N_VPU_LEVELS)) + [N_LEVELS + 1]:
                    store_factor(buf, di, l, lanes, jnp.exp2(coarse_exponent(b, l, di == 1)))

    def recurrence(buf, ci):
        def factor(di, group, h):
            return _unpack(
                e_ref[buf, di, h, _packed_rows(group * c):_packed_rows((group + 1) * c), :])

        inter, scores, own, near = {}, {}, {}, {}
        rows = [pl.ds(chunk_start(di, ci, c), c) for di in range(2)]
        pc = _packed_rows(c)
        words = [pl.ds(chunk_start(di, ci, pc), pc) for di in range(2)]
        for di, _, _, q_ref, v_ref, k_ref, _, _, _ in dirs:
            for h, sl in enumerate(heads):
                q, k = _unpack(q_ref[words[di], sl]), _unpack(k_ref[words[di], sl])
                q_in = q * factor(di, N_LEVELS, h)
                k_out = k * factor(di, N_LEVELS + 1, h)
                st = s_ref[di * n_heads + h]
                inter[di, h] = lax.dot_general(q_in, st.astype(BF16), contract_last,
                                               preferred_element_type=F32)
                ds = lax.dot_general(_unpack(v_ref[words[di], sl]), k_out, contract_rows,
                                     preferred_element_type=F32)
                s_ref[di * n_heads + h] = st * dec_ref[buf, di, :, sl] + ds
                own[di, h] = jnp.sum((q * k).astype(F32), axis=-1, keepdims=True)
                f = factor(di, N_LEVELS - 1, h)
                pair = jnp.sum((q * f).astype(F32) * pltpu.roll((k * f).astype(F32), shifts[di], 0),
                               axis=-1, keepdims=True)
                near[di, h] = jnp.where(query_side[di], pair, 0.0)
                sc = [jnp.zeros((SUBLANES, c), F32)] * (c // SUBLANES)
                for l in range(N_LEVELS - 1):
                    f = factor(di, l, h)
                    a = lax.dot_general(q * f, k * f, contract_last, preferred_element_type=F32)
                    for p in range(c // SUBLANES):
                        if owned[di][l][p]:
                            sc[p] = jnp.where(masks[di][l][p],
                                              a[p * SUBLANES:(p + 1) * SUBLANES], sc[p])
                scores[di, h] = jnp.concatenate(sc, axis=0).astype(BF16)
        for di, _, _, _, v_ref, _, _, o_ref, _ in dirs:
            for h, sl in enumerate(heads):
                v = _unpack(v_ref[words[di], sl])
                intra = jnp.dot(scores[di, h], v, preferred_element_type=F32)
                v32 = v.astype(F32)
                o_ref[rows[di], sl] = (inter[di, h] + intra + own[di, h] * v32
                                       + near[di, h] * pltpu.roll(v32, shifts[di], 0))

    @pl.when(pl.program_id(0) == 0)
    def _():
        decay_factors(0, 0)

    def step_pair(j, carry):
        for half in range(2):
            ci = 2 * j + half
            decay_factors(1 - half, ci + 1)
            recurrence(half, ci)
        return carry

    assert n_chunks % 2 == 0
    lax.fori_loop(0, n_chunks // 2 - 1, step_pair, 0)
    decay_factors(1, n_chunks - 1)
    recurrence(0, n_chunks - 2)
    decay_factors(0, None)
    recurrence(1, n_chunks - 1)


def _scan(nt, q, v, kf, hlf, kb, hlb):
    d = q.shape[1]
    tb = _row_tile(nt)
    nb = nt // tb
    n_heads = d // HEAD_DIM
    w_f, w_b, lv_f, lv_b = _scan_tables()
    fwd = pl.BlockSpec((tb, d), lambda i: (i, 0))
    bwd = pl.BlockSpec((tb, d), lambda i: (nb - 1 - i, 0))
    fwd2 = pl.BlockSpec((_packed_rows(2 * tb), d), lambda i: (i, 0))
    bwd2 = pl.BlockSpec((_packed_rows(2 * tb), d), lambda i: (nb - 1 - i, 0))
    fwdw = pl.BlockSpec((_packed_rows(tb), d), lambda i: (i, 0))
    bwdw = pl.BlockSpec((_packed_rows(tb), d), lambda i: (nb - 1 - i, 0))
    n_chunks = tb // CHUNK
    fwd_next = pl.BlockSpec((_packed_rows(2 * CHUNK), d),
                            lambda i: (jnp.minimum((i + 1) * n_chunks, nb * n_chunks - 1), 0))
    bwd_next = pl.BlockSpec((_packed_rows(2 * CHUNK), d),
                            lambda i: (jnp.maximum((nb - 1 - i) * n_chunks - 1, 0), 0))
    return pl.pallas_call(
        functools.partial(_scan_body, n_heads=n_heads),
        grid=(nb,),
        in_specs=[_resident(w_f.shape), _resident(w_b.shape), _resident(lv_f.shape),
                  _resident(lv_b.shape), fwdw, fwdw, fwdw, fwd2, bwdw, bwdw, bwdw, bwd2,
                  fwd_next, bwd_next],
        out_specs=[fwd, bwd],
        out_shape=[jax.ShapeDtypeStruct((nt, d), F32)] * 2,
        scratch_shapes=[
            pltpu.VMEM((2 * n_heads, HEAD_DIM, HEAD_DIM), F32),
            pltpu.VMEM((2, 2, n_heads, _packed_rows((N_LEVELS + 2) * CHUNK), HEAD_DIM), jnp.uint32),
            pltpu.VMEM((2, 2, 1, d), F32),
        ],
        compiler_params=_params(),
        name="hg_scan",
    )(w_f, w_b, lv_f, lv_b, q, v, kf, hlf, q, v, kb, hlb, hlf, hlb)


def _hg_out_body(of_ref, ob_ref, g_ref, h_ref, gn_ref, wout_ref, gpost_ref, o_ref):
    d = h_ref.shape[1]
    gated = (of_ref[...] + ob_ref[...]) * g_ref[...].astype(F32)
    parts = []
    for h in range(d // HEAD_DIM):
        gh = gated[:, h * HEAD_DIM:(h + 1) * HEAD_DIM]
        parts.append(gh * lax.rsqrt(jnp.mean(gh * gh, axis=-1, keepdims=True) + EPS))
    normed = jnp.concatenate(parts, axis=-1) * gn_ref[...]
    y = jnp.dot(normed.astype(BF16), wout_ref[...], preferred_element_type=F32)
    o_ref[...] = h_ref[...] + _rms(y, gpost_ref[...])


def _hg_out(o_f, o_b, g, h, gn, w_out, gpost):
    nt, d = h.shape
    tm = _row_tile(nt)
    row = pl.BlockSpec((tm, d), lambda i: (i, 0))
    vec = _resident((1, d))
    return pl.pallas_call(
        _hg_out_body,
        grid=(nt // tm,),
        in_specs=[row, row, row, row, vec, _resident((d, d)), vec],
        out_specs=row,
        out_shape=jax.ShapeDtypeStruct((nt, d), F32),
        compiler_params=_params(),
        name="hg_out",
    )(o_f, o_b, g, h, gn, w_out, gpost)


def _boundary_tables(seq_lens, n_rows, tile):
    n_tiles = n_rows // tile
    start = np.full((n_tiles,), -1, np.int32)
    end = np.full((n_tiles,), -1, np.int32)
    base = 0
    for length in seq_lens:
        assert length >= 2 * tile, "at most one sequence boundary per row tile"
        start[base // tile] = base % tile
        last = base + length - 1
        end[last // tile] = last % tile
        base += length
    return jnp.asarray(start), jnp.asarray(end)


def kernel(x_prompt, x_sample, meta_tokens, norm_pre, norm_post, ffn_w_gate, ffn_w_up, ffn_w_down,
           sc_w_in, sc_conv, sc_w_out, hg_w_in, hg_lb_logits, hg_gn, hg_w_out, final_norm):
    d = x_prompt.shape[-1]
    depth = norm_pre.shape[0]
    assert d % HEAD_DIM == 0

    layout = _Layout([x_prompt.shape, x_sample.shape])
    x_flat = [x.reshape(-1, d).astype(F32) for x in (x_prompt, x_sample)]
    start_off, end_off = _boundary_tables(layout.length, layout.n_rows, layout.tile)

    vec = lambda g: g.reshape(1, d).astype(F32)
    w16 = lambda w: w.astype(BF16)
    logits = hg_lb_logits.reshape(2 * depth, -1).astype(F32)

    h = None
    for i in range(depth):
        ffn1 = (vec(norm_pre[i, 0]), vec(norm_post[i, 0]),
                w16(ffn_w_gate[i, 0]), w16(ffn_w_up[i, 0]), w16(ffn_w_down[i, 0]))
        if i == 0:
            h = _ffn_first(layout, x_flat, meta_tokens.astype(F32), *ffn1)
        else:
            h = _ffn(h, *ffn1)
        j = i // 2
        if i % 2 == 0:
            gb, z = _conv_in(h, vec(norm_pre[i, 1]), w16(sc_w_in[j]))
            h = _conv_out(z, gb, h, sc_conv[j].astype(F32), w16(sc_w_out[j]), vec(norm_post[i, 1]),
                          start_off, end_off)
        else:
            q, v, g, kf, kb, hlf, hlb = _hg_in(
                h, vec(norm_pre[i, 1]), w16(hg_w_in[j]), logits, start_off, end_off, layer=i)
            o_f, o_b = _scan(layout.n_rows, q, v, kf, hlf, kb, hlb)
            h = _hg_out(o_f, o_b, g, h, vec(hg_gn[j]), w16(hg_w_out[j]), vec(norm_post[i, 1]))
        ffn2 = (vec(norm_pre[i, 2]), vec(norm_post[i, 2]),
                w16(ffn_w_gate[i, 1]), w16(ffn_w_up[i, 1]), w16(ffn_w_down[i, 1]))
        if i < depth - 1:
            h = _ffn(h, *ffn2)
    y_prompt, y_sample = _ffn_last(layout, h, *ffn2, vec(final_norm))
    return y_prompt.reshape(x_prompt.shape), y_sample.reshape(x_sample.shape)
```

```python
import functools

import numpy as np
import jax
import jax.numpy as jnp
from jax import lax
from jax.experimental import pallas as pl
from jax.experimental.pallas import tpu as pltpu

EPS = 1e-6
N_META = 16
HEAD_DIM = 128
CHUNK = 64
N_LEVELS = 6
N_VPU_LEVELS = 4
NEG_BIG = -2000.0
LOG2E = 1.4426950408889634
MXU_TILE = 256
SUBLANES = 8
VMEM_LIMIT_BYTES = 56 * 1024 * 1024

F32 = jnp.float32
BF16 = jnp.bfloat16


def _rms(x, g):
    return x * lax.rsqrt(jnp.mean(x * x, axis=-1, keepdims=True) + EPS) * g


def _exp_neg(x):
    return jnp.exp2(x * (-LOG2E))


def _silu(x):
    return x / (1.0 + _exp_neg(x))


def _pack(x):
    return pltpu.bitcast(x, jnp.uint32)


def _unpack(x):
    return pltpu.bitcast(x, BF16)


def _packed_rows(rows):
    return rows * jnp.dtype(BF16).itemsize // 4


def _row_tile(n_rows):
    return 512 if n_rows >= 4096 else 128


def _resident(shape):
    nd = len(shape)
    return pl.BlockSpec(shape, lambda i, *_: (0,) * nd, pipeline_mode=pl.Buffered(1))


def _params():
    return pltpu.CompilerParams(dimension_semantics=("arbitrary",),
                                vmem_limit_bytes=VMEM_LIMIT_BYTES)


def _ffn_splits(d_ff):
    if d_ff % MXU_TILE or d_ff < 4 * MXU_TILE:
        return ((0, d_ff),)
    step = 1 * MXU_TILE
    return tuple((lo, min(lo + step, d_ff)) for lo in range(0, d_ff, step))


def _ffn_math(x, gpre_ref, gpost_ref, wg_ref, wu_ref, wd_ref):
    u = _rms(x, gpre_ref[...]).astype(BF16)
    f = None
    for lo, hi in _ffn_splits(wg_ref.shape[1]):
        a = jnp.dot(u, wg_ref[:, lo:hi], preferred_element_type=F32)
        b = jnp.dot(u, wu_ref[:, lo:hi], preferred_element_type=F32)
        hdn = (_silu(a) * b).astype(BF16)
        part = jnp.dot(hdn, wd_ref[lo:hi, :], preferred_element_type=F32)
        f = part if f is None else f + part
    return x + 0.5 * _rms(f, gpost_ref[...])


def _ffn_weight_specs(d, d_ff):
    vec = _resident((1, d))
    return [vec, vec, _resident((d, d_ff)), _resident((d, d_ff)), _resident((d_ff, d))]


def _ffn_body(x_ref, gpre_ref, gpost_ref, wg_ref, wu_ref, wd_ref, o_ref):
    o_ref[...] = _ffn_math(x_ref[...], gpre_ref, gpost_ref, wg_ref, wu_ref, wd_ref)


def _ffn(x, gpre, gpost, wg, wu, wd):
    nt, d = x.shape
    tm = _row_tile(nt)
    row = pl.BlockSpec((tm, d), lambda i: (i, 0))
    return pl.pallas_call(
        _ffn_body,
        grid=(nt // tm,),
        in_specs=[row] + _ffn_weight_specs(d, wg.shape[1]),
        out_specs=row,
        out_shape=jax.ShapeDtypeStruct((nt, d), F32),
        compiler_params=_params(),
        name="ffn",
    )(x, gpre, gpost, wg, wu, wd)


class _Layout:
    def __init__(self, shapes):
        self.arr, self.base, self.first, self.length = [], [], [], []
        self.real = [bn * s for bn, s, _ in shapes]
        row = 0
        for a, (bn, s, _) in enumerate(shapes):
            for b in range(bn):
                self.arr.append(a)
                self.base.append(row)
                self.first.append(b * s)
                self.length.append(N_META + s)
                row += N_META + s
        self.n_real = row
        self.tile = _row_tile(row)
        self.n_rows = -(-row // self.tile) * self.tile
        t = self.tile
        for a, (bn, s, _) in enumerate(shapes):
            assert s % t == 0 and s % 16 == 0 and N_META + s >= 2 * t, (s, t)

    def seq_of(self, row):
        return max(s for s in range(len(self.base)) if self.base[s] <= row)

    def gather_tables(self):
        t = self.tile
        n_tiles = self.n_rows // t
        names = ("blk0", "blk1", "use0", "use1", "src1", "e1", "src2", "e2", "off", "valid")
        cols = {k: np.zeros((n_tiles,), np.int32) for k in names}
        held = [0, 0]
        for i in range(n_tiles):
            u0 = i * t
            starts = [s for s in range(len(self.base)) if u0 <= self.base[s] < u0 + t]
            assert len(starts) <= 1
            off = self.base[starts[0]] - u0 if starts else t
            windows = []
            if off > 0 and u0 < self.n_real:
                windows.append((1, self.seq_of(u0)))
            if starts:
                windows.append((2, starts[0]))
            pairs = {}
            for which, s in windows:
                w = u0 - self.base[s] - N_META + self.first[s]
                blk, e = w // t, w % t
                assert pairs.setdefault(self.arr[s], blk) == blk and e % 16 == 0
                cols["src%d" % which][i], cols["e%d" % which][i] = self.arr[s], e
            for a in range(2):
                held[a] = pairs.get(a, held[a])
                cols["blk%d" % a][i] = held[a]
                cols["use%d" % a][i] = a in pairs
            cols["off"][i] = off
            cols["valid"][i] = min(max(self.n_real - u0, 0), t)
        assert cols["use0"][0] == 1
        return [jnp.asarray(cols[k]) for k in names]

    def scatter_table(self):
        t = self.tile
        start = [self.base[s] + N_META + j * t
                 for s in range(len(self.base)) for j in range((self.length[s] - N_META) // t)]
        assert all(r % 16 == 0 and r + t <= self.n_real for r in start)
        return jnp.asarray(start, jnp.int32)


def _ffn_first_body(b0_ref, b1_ref, u0_ref, u1_ref, s1_ref, e1_ref, s2_ref, e2_ref, off_ref,
                    valid_ref, pa_ref, pb_ref, sa_ref, sb_ref, meta_ref,
                    gpre_ref, gpost_ref, wg_ref, wu_ref, wd_ref, o_ref, pair_ref, x_ref):
    i = pl.program_id(0)
    t = o_ref.shape[0]

    @pl.when(u0_ref[i] == 1)
    def _():
        pair_ref[0, 0:t, :] = pa_ref[...]
        pair_ref[0, t:2 * t, :] = pb_ref[...]

    @pl.when(u1_ref[i] == 1)
    def _():
        pair_ref[1, 0:t, :] = sa_ref[...]
        pair_ref[1, t:2 * t, :] = sb_ref[...]

    def window(src, e):
        return pair_ref[src, pl.ds(pl.multiple_of(e, 16), t), :]

    row = lax.broadcasted_iota(jnp.int32, (t, 1), 0)
    x = jnp.where(row < off_ref[i], window(s1_ref[i], e1_ref[i]), window(s2_ref[i], e2_ref[i]))
    x_ref[...] = jnp.where(row < valid_ref[i], x, 0.0)

    @pl.when(off_ref[i] < t)
    def _():
        x_ref[pl.ds(pl.multiple_of(off_ref[i], 16), N_META), :] = meta_ref[...]

    o_ref[...] = _ffn_math(x_ref[...], gpre_ref, gpost_ref, wg_ref, wu_ref, wd_ref)


def _ffn_first(layout, x_flat, meta, gpre, gpost, wg, wu, wd):
    d = meta.shape[1]
    t = layout.tile
    tables = layout.gather_tables()
    n_blk = [x.shape[0] // t for x in x_flat]

    def block(a, second):
        def index(i, b0, b1, *_):
            return (jnp.clip((b0, b1)[a][i] + second, 0, n_blk[a] - 1), 0)
        return pl.BlockSpec((t, d), index)

    grid_spec = pltpu.PrefetchScalarGridSpec(
        num_scalar_prefetch=len(tables),
        grid=(layout.n_rows // t,),
        in_specs=[block(0, 0), block(0, 1), block(1, 0), block(1, 1), _resident((N_META, d))]
        + _ffn_weight_specs(d, wg.shape[1]),
        out_specs=pl.BlockSpec((t, d), lambda i, *_: (i, 0)),
        scratch_shapes=[pltpu.VMEM((2, 2 * t, d), F32), pltpu.VMEM((t, d), F32)],
    )
    return pl.pallas_call(
        _ffn_first_body,
        grid_spec=grid_spec,
        out_shape=jax.ShapeDtypeStruct((layout.n_rows, d), F32),
        compiler_params=_params(),
        name="ffn_first",
    )(*tables, x_flat[0], x_flat[0], x_flat[1], x_flat[1], meta, gpre, gpost, wg, wu, wd)


def _ffn_last_body(start_ref, h_ref, gpre_ref, gpost_ref, wg_ref, wu_ref, wd_ref,
                   gfin_ref, yp_ref, ys_ref, *, n_first):
    i = pl.program_id(0)
    y = _rms(_ffn_math(h_ref[...], gpre_ref, gpost_ref, wg_ref, wu_ref, wd_ref), gfin_ref[...])

    @pl.when(i < n_first)
    def _():
        yp_ref[...] = y

    @pl.when(i >= n_first)
    def _():
        ys_ref[...] = y


def _ffn_last(layout, h, gpre, gpost, wg, wu, wd, gfinal):
    d = h.shape[1]
    t = layout.tile
    start = layout.scatter_table()
    n_out = [r // t for r in layout.real]
    grid_spec = pltpu.PrefetchScalarGridSpec(
        num_scalar_prefetch=1,
        grid=(n_out[0] + n_out[1],),
        in_specs=[pl.BlockSpec((pl.Element(t), pl.Element(d)),
                               lambda i, start: (pl.multiple_of(start[i], 16), 0))]
        + _ffn_weight_specs(d, wg.shape[1]) + [_resident((1, d))],
        out_specs=[pl.BlockSpec((t, d), lambda i, *_: (jnp.minimum(i, n_out[0] - 1), 0)),
                   pl.BlockSpec((t, d), lambda i, *_: (jnp.maximum(i - n_out[0], 0), 0))],
    )
    return pl.pallas_call(
        functools.partial(_ffn_last_body, n_first=n_out[0]),
        grid_spec=grid_spec,
        out_shape=[jax.ShapeDtypeStruct((r, d), F32) for r in layout.real],
        compiler_params=_params(),
        name="ffn_last",
    )(start, h, gpre, gpost, wg, wu, wd, gfinal)


def _conv_in_body(x_ref, gpre_ref, win_ref, gb_ref, z_ref):
    d = x_ref.shape[1]
    u = _rms(x_ref[...], gpre_ref[...]).astype(BF16)
    p = jnp.dot(u, win_ref[...], preferred_element_type=F32)
    gb_ref[...] = p[:, :d]
    z_ref[...] = p[:, d:2 * d] * p[:, 2 * d:]


def _conv_in(x, gpre, w_in):
    nt, d = x.shape
    tm = _row_tile(nt)
    row = pl.BlockSpec((tm, d), lambda i: (i, 0))
    return pl.pallas_call(
        _conv_in_body,
        grid=(nt // tm,),
        in_specs=[row, _resident((1, d)), _resident((d, 3 * d))],
        out_specs=[row, row],
        out_shape=[jax.ShapeDtypeStruct((nt, d), F32)] * 2,
        compiler_params=_params(),
        name="conv_in",
    )(x, gpre, w_in)


def _conv_out_body(so_ref, eo_ref, z_ref, zp_ref, zn_ref, gb_ref, h_ref, wc_ref, wout_ref,
                   gpost_ref, o_ref):
    i = pl.program_id(0)
    tm = z_ref.shape[0]
    z = z_ref[...]
    row = lax.broadcasted_iota(jnp.int32, (tm, 1), 0)
    edge = lax.broadcasted_iota(jnp.int32, (SUBLANES, 1), 0)
    down, up = pltpu.roll(z, 1, 0), pltpu.roll(z, tm - 1, 0)
    zm1 = jnp.concatenate(
        [jnp.where(edge == 0, zp_ref[7:8, :], down[:SUBLANES]), down[SUBLANES:]], axis=0)
    zp1 = jnp.concatenate(
        [up[:-SUBLANES], jnp.where(edge == SUBLANES - 1, zn_ref[0:1, :], up[-SUBLANES:])], axis=0)
    zm1 = jnp.where(row == so_ref[i], 0.0, zm1)
    zp1 = jnp.where(row == eo_ref[i], 0.0, zp1)
    conv = wc_ref[0:1, :] * zm1 + wc_ref[1:2, :] * z + wc_ref[2:3, :] * zp1
    y = jnp.dot((gb_ref[...] * conv).astype(BF16), wout_ref[...], preferred_element_type=F32)
    o_ref[...] = h_ref[...] + _rms(y, gpost_ref[...])


def _conv_out(z, gb, h, w_conv, w_out, gpost, start_off, end_off):
    nt, d = z.shape
    tm = _row_tile(nt)
    sub = tm // 8
    n_sub = nt // 8
    row = pl.BlockSpec((tm, d), lambda i, *_: (i, 0))
    prev8 = pl.BlockSpec((8, d), lambda i, *_: (jnp.maximum(i * sub - 1, 0), 0))
    next8 = pl.BlockSpec((8, d), lambda i, *_: (jnp.minimum((i + 1) * sub, n_sub - 1), 0))
    grid_spec = pltpu.PrefetchScalarGridSpec(
        num_scalar_prefetch=2,
        grid=(nt // tm,),
        in_specs=[row, prev8, next8, row, row, _resident(w_conv.shape), _resident((d, d)),
                  _resident((1, d))],
        out_specs=row,
    )
    return pl.pallas_call(
        _conv_out_body,
        grid_spec=grid_spec,
        out_shape=jax.ShapeDtypeStruct((nt, d), F32),
        compiler_params=_params(),
        name="conv_out",
    )(start_off, end_off, z, z, z, gb, h, w_conv, w_out, gpost)


def _log_gates(z, lb):
    t = _exp_neg(jnp.abs(z))
    pos = z >= 0.0
    r = 1.0 / (1.0 + t)
    f = jnp.where(pos, 1.0 + lb * t, lb + t) * r
    log2f = jnp.maximum(jnp.log(f) * LOG2E, NEG_BIG)
    key = (1.0 - lb) * (jnp.where(pos, t, 1.0) * r)
    return log2f, key


def _split_hi_lo(x):
    hi = x.astype(BF16)
    lo = (x - hi.astype(F32)).astype(BF16)
    return hi, lo


def _hg_in_body(so_ref, eo_ref, x_ref, gpre_ref, win_ref, logit_ref,
                q_ref, v_ref, g_ref, kf_ref, kb_ref, hlf_ref, hlb_ref, *, layer):
    i = pl.program_id(0)
    tm, d = x_ref.shape
    u = _rms(x_ref[...], gpre_ref[...]).astype(BF16)
    depth = logit_ref.shape[0] // 2
    row = lax.broadcasted_iota(jnp.int32, (tm, 1), 0)

    def proj(j, cols):
        return jnp.dot(u, win_ref[:, j * d + cols.start:j * d + cols.stop],
                       preferred_element_type=F32)

    for cols in (slice(j, min(j + MXU_TILE, d)) for j in range(0, d, MXU_TILE)):
        for direction, (k_ref, hl_ref, off_ref) in enumerate(
                ((kf_ref, hlf_ref, so_ref), (kb_ref, hlb_ref, eo_ref))):
            if direction == 0:
                q_ref[:, cols] = _pack(_silu(proj(0, cols)).astype(BF16))
            else:
                g_ref[:, cols] = _pack(_silu(proj(4, cols)).astype(BF16))
            logits = [logit_ref[2 * j + direction:2 * j + direction + 1, cols]
                      for j in range(depth)]
            m = functools.reduce(jnp.maximum, logits)
            e = [jnp.exp(l - m) for l in logits]
            lb = sum(e[1:layer + 1]) / sum(e)
            logf, key = _log_gates(proj(2 + direction, cols), lb)
            logf = jnp.where(row == off_ref[i], NEG_BIG, logf)
            k_ref[:, cols] = _pack(key.astype(BF16))
            hi, lo = _split_hi_lo(logf)
            for c in range(tm // CHUNK):
                pc = _packed_rows(CHUNK)
                hl_ref[2 * c * pc:(2 * c + 1) * pc, cols] = _pack(hi[c * CHUNK:(c + 1) * CHUNK])
                hl_ref[(2 * c + 1) * pc:(2 * c + 2) * pc, cols] = _pack(lo[c * CHUNK:(c + 1) * CHUNK])
        v_ref[:, cols] = _pack(proj(1, cols).astype(BF16))


def _hg_in(x, gpre, w_in, logits, start_off, end_off, layer):
    nt, d = x.shape
    tm = _row_tile(nt)
    row = pl.BlockSpec((tm, d), lambda i, *_: (i, 0))
    row2 = pl.BlockSpec((_packed_rows(2 * tm), d), lambda i, *_: (i, 0))
    words = pl.BlockSpec((_packed_rows(tm), d), lambda i, *_: (i, 0))
    packed = jax.ShapeDtypeStruct((_packed_rows(nt), d), jnp.uint32)
    grid_spec = pltpu.PrefetchScalarGridSpec(
        num_scalar_prefetch=2,
        grid=(nt // tm,),
        in_specs=[row, _resident((1, d)), _resident((d, 5 * d)), _resident(logits.shape)],
        out_specs=[words] * 5 + [row2, row2],
    )
    return pl.pallas_call(
        functools.partial(_hg_in_body, layer=layer),
        grid_spec=grid_spec,
        out_shape=[packed] * 5
        + [jax.ShapeDtypeStruct((_packed_rows(2 * nt), d), jnp.uint32)] * 2,
        compiler_params=_params(),
        name="hg_in",
    )(start_off, end_off, x, gpre, w_in, logits)


def _scan_tables():
    c = CHUNK
    w = np.zeros((N_LEVELS + 1, c, c), np.float32)
    for l in range(N_LEVELS):
        m = c >> (l + 1)
        for r in range(c):
            n = (r // (2 * m)) * 2 * m + m - 1
            lo, hi = (r, n) if r <= n else (n, r)
            w[l, r, lo + 1:hi + 1] = 1.0
    for r in range(c):
        w[N_LEVELS, r, :r + 1] = 1.0
    keep = [N_LEVELS] + list(range(N_VPU_LEVELS, N_LEVELS))
    w_f = np.tile(w[keep].reshape(-1, c), (1, 2))
    w_b = np.tile(w[keep][:, ::-1, ::-1].reshape(-1, c), (1, 2))
    lv_f, lv_b = _level_tables()
    return jnp.asarray(w_f, BF16), jnp.asarray(w_b, BF16), jnp.asarray(lv_f), jnp.asarray(lv_b)


def _level_tables():
    c = CHUNK
    level = np.full((c, c), N_LEVELS + 1, np.int32)
    for l in range(N_LEVELS):
        m = c >> (l + 1)
        for t in range(c):
            for s in range(c):
                if t // (2 * m) == s // (2 * m) and (t // m) % 2 == 1 and (s // m) % 2 == 0:
                    level[t, s] = l
    level[np.arange(c), np.arange(c)] = N_LEVELS
    return level, level[::-1, ::-1].copy()


def _reference_rows(level, backward):
    c = CHUNK
    if level == N_LEVELS + 1:
        return [(0, c, 0 if backward else c - 1)]
    m = c >> (level + 1)
    return [(g, g + 2 * m, g + (m if backward else m - 1)) for g in range(0, c, 2 * m)]


def _scan_body(wf_ref, wb_ref, lvf_ref, lvb_ref,
               qf_ref, vf_ref, kf_ref, hlf_ref,
               qb_ref, vb_ref, kb_ref, hlb_ref, nhlf_ref, nhlb_ref,
               of_ref, ob_ref, s_ref, e_ref, dec_ref, *, n_heads):
    tb, d = of_ref.shape
    n_chunks = tb // CHUNK
    c = CHUNK

    @pl.when(pl.program_id(0) == 0)
    def _():
        s_ref[...] = jnp.zeros_like(s_ref)

    dirs = (
        (0, wf_ref, lvf_ref, qf_ref, vf_ref, kf_ref, hlf_ref, of_ref, c - 1),
        (1, wb_ref, lvb_ref, qb_ref, vb_ref, kb_ref, hlb_ref, ob_ref, 0),
    )
    masks = []
    for _, _, lv_ref, *_ in dirs:
        lv = lv_ref[...]
        masks.append([[lv[p:p + SUBLANES] == l for p in range(0, c, SUBLANES)]
                      for l in range(N_LEVELS + 1)])
    owned = [[[bool((lv_np[p:p + SUBLANES] == l).any()) for p in range(0, c, SUBLANES)]
              for l in range(N_LEVELS + 1)] for lv_np in _level_tables()]
    parity = lax.broadcasted_iota(jnp.int32, (c, 1), 0) & 1
    query_side = [parity == 1, parity == 0]
    shifts = [1, c - 1]
    contract_last = (((1,), (1,)), ((), ()))
    contract_rows = (((0,), (0,)), ((), ()))
    heads = [slice(h * HEAD_DIM, (h + 1) * HEAD_DIM) for h in range(n_heads)]
    lane_blocks = [slice(j, min(j + MXU_TILE, d)) for j in range(0, d, MXU_TILE)]

    def chunk_start(di, ci, rows_per_chunk):
        start = (ci if di == 0 else n_chunks - 1 - ci) * rows_per_chunk
        return start if isinstance(start, int) else pl.multiple_of(start, rows_per_chunk)

    def store_factor(buf, di, group, lanes, value):
        for h in range(lanes.start // HEAD_DIM, lanes.stop // HEAD_DIM):
            lo = h * HEAD_DIM - lanes.start
            e_ref[buf, di, h, _packed_rows(group * c):_packed_rows((group + 1) * c), :] = (
                _pack(value[:, lo:lo + HEAD_DIM].astype(BF16)))

    def coarse_exponent(b, level, backward):
        tiles = []
        for lo, hi, n in _reference_rows(level, backward):
            ref = b[n:n + 1, :]
            for p in range(lo, hi, SUBLANES):
                rows = b[p:p + SUBLANES]
                before = p + SUBLANES - 1 < n if backward else p + SUBLANES - 1 <= n
                after = p >= n if backward else p > n
                if before:
                    tiles.append(rows - ref if backward else ref - rows)
                elif after:
                    tiles.append(ref - rows if backward else rows - ref)
                else:
                    tiles.append(-jnp.abs(rows - ref))
        return jnp.concatenate(tiles, axis=0)

    def decay_factors(buf, ci):
        for di, w_ref, _, _, _, _, hl_ref, _, total_row in dirs:
            pc2 = _packed_rows(2 * c)
            if ci is None:
                hl_ref, start = (nhlf_ref, nhlb_ref)[di], 0
            else:
                start = chunk_start(di, ci, pc2)
            for lanes in lane_blocks:
                x = jnp.dot(w_ref[...], _unpack(hl_ref[pl.ds(start, pc2), lanes]),
                            preferred_element_type=F32)
                b = x[0:c]
                e_b = jnp.exp2(b)
                store_factor(buf, di, N_LEVELS, lanes, e_b)
                dec_ref[buf, di, :, lanes] = e_b[total_row:total_row + 1, :]
                for j, l in enumerate(range(N_VPU_LEVELS, N_LEVELS)):
                    store_factor(buf, di, l, lanes, jnp.exp2(x[(j + 1) * c:(j + 2) * c]))
                for l in list(range(N_VPU_LEVELS)) + [N_LEVELS + 1]:
                    store_factor(buf, di, l, lanes, jnp.exp2(coarse_exponent(b, l, di == 1)))

    def recurrence(buf, ci):
        def factor(di, group, h):
            return _unpack(
                e_ref[buf, di, h, _packed_rows(group * c):_packed_rows((group + 1) * c), :])

        inter, scores, own, near = {}, {}, {}, {}
        rows = [pl.ds(chunk_start(di, ci, c), c) for di in range(2)]
        pc = _packed_rows(c)
        words = [pl.ds(chunk_start(di, ci, pc), pc) for di in range(2)]
        for di, _, _, q_ref, v_ref, k_ref, _, _, _ in dirs:
            for h, sl in enumerate(heads):
                q, k = _unpack(q_ref[words[di], sl]), _unpack(k_ref[words[di], sl])
                q_in = q * factor(di, N_LEVELS, h)
                k_out = k * factor(di, N_LEVELS + 1, h)
                st = s_ref[di * n_heads + h]
                inter[di, h] = lax.dot_general(q_in, st.astype(BF16), contract_last,
                                               preferred_element_type=F32)
                ds = lax.dot_general(_unpack(v_ref[words[di], sl]), k_out, contract_rows,
                                     preferred_element_type=F32)
                s_ref[di * n_heads + h] = st * dec_ref[buf, di, :, sl] + ds
                own[di, h] = jnp.sum((q * k).astype(F32), axis=-1, keepdims=True)
                f = factor(di, N_LEVELS - 1, h)
                pair = jnp.sum((q * f).astype(F32) * pltpu.roll((k * f).astype(F32), shifts[di], 0),
                               axis=-1, keepdims=True)
                near[di, h] = jnp.where(query_side[di], pair, 0.0)
                sc = [jnp.zeros((SUBLANES, c), F32)] * (c // SUBLANES)
                for l in range(N_LEVELS - 1):
                    f = factor(di, l, h)
                    a = lax.dot_general(q * f, k * f, contract_last, preferred_element_type=F32)
                    for p in range(c // SUBLANES):
                        if owned[di][l][p]:
                            sc[p] = jnp.where(masks[di][l][p],
                                              a[p * SUBLANES:(p + 1) * SUBLANES], sc[p])
                scores[di, h] = jnp.concatenate(sc, axis=0).astype(BF16)
        for di, _, _, _, v_ref, _, _, o_ref, _ in dirs:
            for h, sl in enumerate(heads):
                v = _unpack(v_ref[words[di], sl])
                intra = jnp.dot(scores[di, h], v, preferred_element_type=F32)
                v32 = v.astype(F32)
                o_ref[rows[di], sl] = (inter[di, h] + intra + own[di, h] * v32
                                       + near[di, h] * pltpu.roll(v32, shifts[di], 0))

    @pl.when(pl.program_id(0) == 0)
    def _():
        decay_factors(0, 0)

    def step_pair(j, carry):
        for half in range(2):
            ci = 2 * j + half
            decay_factors(1 - half, ci + 1)
            recurrence(half, ci)
        return carry

    assert n_chunks % 2 == 0
    lax.fori_loop(0, n_chunks // 2 - 1, step_pair, 0)
    decay_factors(1, n_chunks - 1)
    recurrence(0, n_chunks - 2)
    decay_factors(0, None)
    recurrence(1, n_chunks - 1)


def _scan(nt, q, v, kf, hlf, kb, hlb):
    d = q.shape[1]
    tb = _row_tile(nt)
    nb = nt // tb
    n_heads = d // HEAD_DIM
    w_f, w_b, lv_f, lv_b = _scan_tables()
    fwd = pl.BlockSpec((tb, d), lambda i: (i, 0))
    bwd = pl.BlockSpec((tb, d), lambda i: (nb - 1 - i, 0))
    fwd2 = pl.BlockSpec((_packed_rows(2 * tb), d), lambda i: (i, 0))
    bwd2 = pl.BlockSpec((_packed_rows(2 * tb), d), lambda i: (nb - 1 - i, 0))
    fwdw = pl.BlockSpec((_packed_rows(tb), d), lambda i: (i, 0))
    bwdw = pl.BlockSpec((_packed_rows(tb), d), lambda i: (nb - 1 - i, 0))
    n_chunks = tb // CHUNK
    fwd_next = pl.BlockSpec((_packed_rows(2 * CHUNK), d),
                            lambda i: (jnp.minimum((i + 1) * n_chunks, nb * n_chunks - 1), 0))
    bwd_next = pl.BlockSpec((_packed_rows(2 * CHUNK), d),
                            lambda i: (jnp.maximum((nb - 1 - i) * n_chunks - 1, 0), 0))
    return pl.pallas_call(
        functools.partial(_scan_body, n_heads=n_heads),
        grid=(nb,),
        in_specs=[_resident(w_f.shape), _resident(w_b.shape), _resident(lv_f.shape),
                  _resident(lv_b.shape), fwdw, fwdw, fwdw, fwd2, bwdw, bwdw, bwdw, bwd2,
                  fwd_next, bwd_next],
        out_specs=[fwd, bwd],
        out_shape=[jax.ShapeDtypeStruct((nt, d), F32)] * 2,
        scratch_shapes=[
            pltpu.VMEM((2 * n_heads, HEAD_DIM, HEAD_DIM), F32),
            pltpu.VMEM((2, 2, n_heads, _packed_rows((N_LEVELS + 2) * CHUNK), HEAD_DIM), jnp.uint32),
            pltpu.VMEM((2, 2, 1, d), F32),
        ],
        compiler_params=_params(),
        name="hg_scan",
    )(w_f, w_b, lv_f, lv_b, q, v, kf, hlf, q, v, kb, hlb, hlf, hlb)


def _hg_out_body(of_ref, ob_ref, g_ref, h_ref, gn_ref, wout_ref, gpost_ref, o_ref):
    d = h_ref.shape[1]
    gated = (of_ref[...] + ob_ref[...]) * _unpack(g_ref[...]).astype(F32)
    parts = []
    for h in range(d // HEAD_DIM):
        gh = gated[:, h * HEAD_DIM:(h + 1) * HEAD_DIM]
        parts.append(gh * lax.rsqrt(jnp.mean(gh * gh, axis=-1, keepdims=True) + EPS))
    normed = jnp.concatenate(parts, axis=-1) * gn_ref[...]
    y = jnp.dot(normed.astype(BF16), wout_ref[...], preferred_element_type=F32)
    o_ref[...] = h_ref[...] + _rms(y, gpost_ref[...])


def _hg_out(o_f, o_b, g, h, gn, w_out, gpost):
    nt, d = h.shape
    tm = _row_tile(nt)
    row = pl.BlockSpec((tm, d), lambda i: (i, 0))
    words = pl.BlockSpec((_packed_rows(tm), d), lambda i: (i, 0))
    vec = _resident((1, d))
    return pl.pallas_call(
        _hg_out_body,
        grid=(nt // tm,),
        in_specs=[row, row, words, row, vec, _resident((d, d)), vec],
        out_specs=row,
        out_shape=jax.ShapeDtypeStruct((nt, d), F32),
        compiler_params=_params(),
        name="hg_out",
    )(o_f, o_b, g, h, gn, w_out, gpost)


def _boundary_tables(seq_lens, n_rows, tile):
    n_tiles = n_rows // tile
    start = np.full((n_tiles,), -1, np.int32)
    end = np.full((n_tiles,), -1, np.int32)
    base = 0
    for length in seq_lens:
        assert length >= 2 * tile, "at most one sequence boundary per row tile"
        start[base // tile] = base % tile
        last = base + length - 1
        end[last // tile] = last % tile
        base += length
    return jnp.asarray(start), jnp.asarray(end)


def kernel(x_prompt, x_sample, meta_tokens, norm_pre, norm_post, ffn_w_gate, ffn_w_up, ffn_w_down,
           sc_w_in, sc_conv, sc_w_out, hg_w_in, hg_lb_logits, hg_gn, hg_w_out, final_norm):
    d = x_prompt.shape[-1]
    depth = norm_pre.shape[0]
    assert d % HEAD_DIM == 0

    layout = _Layout([x_prompt.shape, x_sample.shape])
    x_flat = [x.reshape(-1, d).astype(F32) for x in (x_prompt, x_sample)]
    start_off, end_off = _boundary_tables(layout.length, layout.n_rows, layout.tile)

    vec = lambda g: g.reshape(1, d).astype(F32)
    w16 = lambda w: w.astype(BF16)
    logits = hg_lb_logits.reshape(2 * depth, -1).astype(F32)

    h = None
    for i in range(depth):
        ffn1 = (vec(norm_pre[i, 0]), vec(norm_post[i, 0]),
                w16(ffn_w_gate[i, 0]), w16(ffn_w_up[i, 0]), w16(ffn_w_down[i, 0]))
        if i == 0:
            h = _ffn_first(layout, x_flat, meta_tokens.astype(F32), *ffn1)
        else:
            h = _ffn(h, *ffn1)
        j = i // 2
        if i % 2 == 0:
            gb, z = _conv_in(h, vec(norm_pre[i, 1]), w16(sc_w_in[j]))
            h = _conv_out(z, gb, h, sc_conv[j].astype(F32), w16(sc_w_out[j]), vec(norm_post[i, 1]),
                          start_off, end_off)
        else:
            q, v, g, kf, kb, hlf, hlb = _hg_in(
                h, vec(norm_pre[i, 1]), w16(hg_w_in[j]), logits, start_off, end_off, layer=i)
            o_f, o_b = _scan(layout.n_rows, q, v, kf, hlf, kb, hlb)
            h = _hg_out(o_f, o_b, g, h, vec(hg_gn[j]), w16(hg_w_out[j]), vec(norm_post[i, 1]))
        ffn2 = (vec(norm_pre[i, 2]), vec(norm_post[i, 2]),
                w16(ffn_w_gate[i, 1]), w16(ffn_w_up[i, 1]), w16(ffn_w_down[i, 1]))
        if i < depth - 1:
            h = _ffn(h, *ffn2)
    y_prompt, y_sample = _ffn_last(layout, h, *ffn2, vec(final_norm))
    return y_prompt.reshape(x_prompt.shape), y_sample.reshape(x_sample.shape)
```
